```python
import jax, jax.numpy as jnp
from jax import lax
import numpy as np

D_MODEL = 2048
BATCH = 4
SEQ = 2048
DEPTH = 2
DEC_BATCH = 128
DEC_SEQ = 4
PAST_LEN = 16384
PAGE_SIZE = 128

HGRN_WIDTH = D_MODEL // 2
HGRN_HEAD_DIM = 128
HGRN_HEADS = HGRN_WIDTH // HGRN_HEAD_DIM
HGRN_CHUNK = 64
POOL_WIDTH = D_MODEL - HGRN_WIDTH
POOL_WINDOWS = (2, 4, 8, 16)
POOL_GROUPS = len(POOL_WINDOWS)
POOL_GROUP_DIM = POOL_WIDTH // POOL_GROUPS
POOL_BUF = max(POOL_WINDOWS) - 1
IN_PROJ_WIDTH = 4 * HGRN_WIDTH + POOL_WIDTH
PEER_HEADS = 8
PEER_N_KEYS = 128
PEER_N_EXPERTS = PEER_N_KEYS ** 2
PEER_TOPK = 16
PEER_QUERY_DIM = 256
PEER_HALF = PEER_QUERY_DIM // 2
PEER_BLOCK = 128
N_MOD = 6
EPS = 1e-6

kernel_name = "hgrn2_pool_peer_hybrid_step"


def rmsnorm(x, g):
    xf = x.astype(jnp.float32)
    y = xf * lax.rsqrt(jnp.mean(xf * xf, axis=-1, keepdims=True) + EPS)
    return (y * g.astype(jnp.float32)).astype(x.dtype)


def modulate(x, shift, scale):
    return x * (1 + scale[:, None]) + shift[:, None]


def hgrn2_chunked(q, k, v, log_f, s0, chunk):
    dt = v.dtype
    B, T, H, DK = q.shape
    DV = v.shape[-1]
    n = -(-T // chunk)
    pad = n * chunk - T

    def prep(a):
        a = jnp.pad(a.astype(jnp.float32), ((0, 0), (0, pad), (0, 0), (0, 0)))
        return a.reshape(B, n, chunk, H, a.shape[-1]).swapaxes(0, 1)

    mask = jnp.tril(jnp.ones((chunk, chunk), dtype=bool))[None, :, :, None, None]

    def step(S, inp):
        qc, kc, vc, gc = inp
        G = jnp.cumsum(gc, axis=1)
        o_inter = jnp.einsum('bthk,bhkv->bthv', qc * jnp.exp(G), S)
        diff = G[:, :, None] - G[:, None, :]
        decay = jnp.exp(jnp.where(mask, diff, -jnp.inf))
        A = jnp.einsum('bthk,bshk,btshk->bths', qc, kc, decay)
        o_intra = jnp.einsum('bths,bshv->bthv', A, vc)
        g_end = G[:, -1]
        k_dec = kc * jnp.exp(g_end[:, None] - G)
        S_new = jnp.exp(g_end)[..., None] * S + jnp.einsum('bshk,bshv->bhkv', k_dec, vc)
        return S_new, o_inter + o_intra

    S_fin, o = lax.scan(step, s0.astype(jnp.float32), (prep(q), prep(k), prep(v), prep(log_f)))
    o = o.swapaxes(0, 1).reshape(B, n * chunk, H, DV)[:, :T]
    return o.astype(dt), S_fin.astype(s0.dtype)


def multiscale_pool(v, buf, start_pos):
    B, T, W = v.shape
    full = jnp.concatenate([buf.astype(v.dtype), v], axis=1)
    cs = jnp.pad(jnp.cumsum(full.astype(jnp.float32), axis=1), ((0, 0), (1, 0), (0, 0)))
    pos = start_pos + jnp.arange(T, dtype=jnp.int32)
    vf = v.astype(jnp.float32)
    outs = []
    for gi, w in enumerate(POOL_WINDOWS):
        sl = slice(gi * POOL_GROUP_DIM, (gi + 1) * POOL_GROUP_DIM)
        hi = cs[:, POOL_BUF + 1:POOL_BUF + 1 + T, sl]
        lo = cs[:, POOL_BUF + 1 - w:POOL_BUF + 1 - w + T, sl]
        cnt = jnp.minimum(w, pos + 1).astype(jnp.float32)
        outs.append((hi - lo) / cnt[None, :, None] - vf[..., sl])
    pooled = jnp.stack(outs, axis=2)
    return pooled, full[:, -POOL_BUF:]


def peer_ffn(h, wq, keys, u_tab, v_tab):
    B, T, D = h.shape
    N = B * T
    xt = h.reshape(N, D)
    qry = (xt @ wq).reshape(N, PEER_HEADS, 2, PEER_HALF).astype(jnp.float32)
    qry = qry * lax.rsqrt(jnp.mean(qry * qry, axis=-1, keepdims=True) + EPS)
    scores = jnp.einsum('nhpd,hpkd->nhpk', qry, keys.astype(jnp.float32))
    s1, i1 = lax.top_k(scores[:, :, 0], PEER_TOPK)
    s2, i2 = lax.top_k(scores[:, :, 1], PEER_TOPK)
    cand = (s1[..., :, None] + s2[..., None, :]).reshape(N, PEER_HEADS, PEER_TOPK * PEER_TOPK)
    cidx = (i1[..., :, None] * PEER_N_KEYS + i2[..., None, :]).reshape(N, PEER_HEADS, PEER_TOPK * PEER_TOPK)
    top_s, top_pos = lax.top_k(cand, PEER_TOPK)
    eidx = jnp.take_along_axis(cidx, top_pos, axis=-1)
    gate = jax.nn.softmax(top_s, axis=-1).astype(h.dtype)
    nb = -(-N // PEER_BLOCK)
    pad = nb * PEER_BLOCK - N
    xb = jnp.pad(xt, ((0, pad), (0, 0))).reshape(nb, PEER_BLOCK, D)
    eb = jnp.pad(eidx, ((0, pad), (0, 0), (0, 0))).reshape(nb, PEER_BLOCK, PEER_HEADS, PEER_TOPK)
    gb = jnp.pad(gate, ((0, pad), (0, 0), (0, 0))).reshape(nb, PEER_BLOCK, PEER_HEADS, PEER_TOPK)

    def block(args):
        xs, es, gs = args
        u = u_tab[es]
        a = jax.nn.gelu(jnp.einsum('bd,bhkd->bhk', xs, u), approximate=False) * gs
        return jnp.einsum('bhk,bhkd->bd', a, v_tab[es])

    out = lax.map(block, (xb, eb, gb)).reshape(nb * PEER_BLOCK, D)[:N]
    return out.reshape(B, T, D)


def hybrid_layer(x, c, hgrn_state, pool_buf, start_pos, lb, w_ada, b_ada, norm1_g, norm2_g,
                 w_in, w_out, hgrn_norm_g, pool_w, pool_b, pool_scale,
                 peer_wq, peer_keys, peer_u, peer_v):
    B, T, D = x.shape
    mod = jax.nn.silu(c) @ w_ada + b_ada
    sh1, sc1, g1, sh2, sc2, g2 = jnp.split(mod, N_MOD, axis=-1)
    h = modulate(rmsnorm(x, norm1_g), sh1, sc1)
    z = h @ w_in
    zq, zf, zi, zg, zp = jnp.split(
        z, [HGRN_WIDTH, 2 * HGRN_WIDTH, 3 * HGRN_WIDTH, 4 * HGRN_WIDTH], axis=-1)

    def heads(a):
        return a.reshape(B, T, HGRN_HEADS, HGRN_HEAD_DIM)

    q = jax.nn.silu(heads(zq))
    zf32 = heads(zf).astype(jnp.float32)
    lbh = lb.reshape(HGRN_HEADS, HGRN_HEAD_DIM).astype(jnp.float32)
    log_f = jnp.logaddexp(jnp.log(lbh), jnp.log1p(-lbh) + jax.nn.log_sigmoid(zf32))
    k = (1 - lbh) * jax.nn.sigmoid(-zf32)
    o, new_state = hgrn2_chunked(q, k, heads(zi), log_f, hgrn_state, min(HGRN_CHUNK, T))
    o_a = (rmsnorm(o, hgrn_norm_g) * jax.nn.silu(heads(zg))).reshape(B, T, HGRN_WIDTH)

    pooled, new_buf = multiscale_pool(zp, pool_buf, start_pos)
    pooled = pooled.astype(x.dtype)
    o_b = (jnp.einsum('btgi,gio->btgo', pooled, pool_w)
           + pool_b.reshape(POOL_GROUPS, POOL_GROUP_DIM)) * pool_scale.reshape(POOL_GROUPS, POOL_GROUP_DIM)
    o_b = o_b.reshape(B, T, POOL_WIDTH)

    mix = jnp.concatenate([o_a, o_b], axis=-1) @ w_out
    x = x + g1[:, None] * mix

    h2 = modulate(rmsnorm(x, norm2_g), sh2, sc2)
    x = x + g2[:, None] * peer_ffn(h2, peer_wq, peer_keys, peer_u, peer_v)
    return x, new_state, new_buf


def setup_inputs(seed: int = 0) -> dict:
    key = jax.random.key(seed)
    ks = jax.random.split(key, 24)

    def nrm(k, shape, scale):
        return jax.random.normal(k, shape, jnp.float32) * scale

    D = D_MODEL
    return {
        "x_prompt": nrm(ks[0], (BATCH, SEQ, D), 1.0),
        "x_sample": nrm(ks[1], (DEC_BATCH, DEC_SEQ, D), 1.0),
        "c_prompt": nrm(ks[2], (BATCH, D), 1.0),
        "c_sample": nrm(ks[3], (DEC_BATCH, D), 1.0),
        "state_hgrn": nrm(ks[4], (DEPTH, DEC_BATCH, HGRN_HEADS, HGRN_HEAD_DIM, HGRN_HEAD_DIM), 0.2),
        "state_pool": nrm(ks[5], (DEPTH, DEC_BATCH, POOL_BUF, POOL_WIDTH), 1.0),
        "w_ada": nrm(ks[6], (DEPTH, D, N_MOD * D), 0.5 * D ** -0.5),
        "b_ada": nrm(ks[7], (DEPTH, N_MOD * D), 0.01),
        "norm1_g": 1.0 + nrm(ks[8], (DEPTH, D), 0.02),
        "norm2_g": 1.0 + nrm(ks[9], (DEPTH, D), 0.02),
        "w_in": nrm(ks[10], (DEPTH, D, IN_PROJ_WIDTH), D ** -0.5),
        "w_out": nrm(ks[11], (DEPTH, D, D), D ** -0.5),
        "lb_logits": nrm(ks[12], (DEPTH, HGRN_WIDTH), 1.0),
        "hgrn_norm_g": 1.0 + nrm(ks[13], (DEPTH, HGRN_HEAD_DIM), 0.02),
        "pool_w": nrm(ks[14], (DEPTH, POOL_GROUPS, POOL_GROUP_DIM, POOL_GROUP_DIM), POOL_GROUP_DIM ** -0.5),
        "pool_b": nrm(ks[15], (DEPTH, POOL_WIDTH), 0.01),
        "pool_scale": 1.0 + nrm(ks[16], (DEPTH, POOL_WIDTH), 0.02),
        "peer_wq": nrm(ks[17], (DEPTH, D, PEER_HEADS * PEER_QUERY_DIM), D ** -0.5),
        "peer_keys": nrm(ks[18], (DEPTH, PEER_HEADS, 2, PEER_N_KEYS, PEER_HALF), PEER_HALF ** -0.5),
        "peer_u": nrm(ks[19], (DEPTH, PEER_N_EXPERTS, D), D ** -0.5),
        "peer_v": nrm(ks[20], (DEPTH, PEER_N_EXPERTS, D), PEER_HEADS ** -0.5),
        "final_g": 1.0 + nrm(ks[21], (D,), 0.02),
        "w_ada_final": nrm(ks[22], (D, 2 * D), 0.5 * D ** -0.5),
        "b_ada_final": nrm(ks[23], (2 * D,), 0.01),
    }


def reference(x_prompt, x_sample, c_prompt, c_sample, state_hgrn, state_pool,
              w_ada, b_ada, norm1_g, norm2_g, w_in, w_out, lb_logits, hgrn_norm_g,
              pool_w, pool_b, pool_scale, peer_wq, peer_keys, peer_u, peer_v,
              final_g, w_ada_final, b_ada_final):
    lb_all = jnp.cumsum(jax.nn.softmax(lb_logits.astype(jnp.float32), axis=0), axis=0)
    lb_all = lb_all - lb_all[0:1]

    zero_state = jnp.zeros((BATCH, HGRN_HEADS, HGRN_HEAD_DIM, HGRN_HEAD_DIM), x_prompt.dtype)
    zero_buf = jnp.zeros((BATCH, POOL_BUF, POOL_WIDTH), x_prompt.dtype)

    xp, xs = x_prompt, x_sample
    sp_list, bp_list, ss_list, bs_list = [], [], [], []
    for l in range(DEPTH):
        w = (lb_all[l], w_ada[l], b_ada[l], norm1_g[l], norm2_g[l], w_in[l], w_out[l],
             hgrn_norm_g[l], pool_w[l], pool_b[l], pool_scale[l],
             peer_wq[l], peer_keys[l], peer_u[l], peer_v[l])
        xp, sp, bp = hybrid_layer(xp, c_prompt, zero_state, zero_buf, 0, *w)
        xs, ss, bs = hybrid_layer(xs, c_sample, state_hgrn[l], state_pool[l], PAST_LEN, *w)
        sp_list.append(sp)
        bp_list.append(bp)
        ss_list.append(ss)
        bs_list.append(bs)

    def final_norm(x, c):
        mod = jax.nn.silu(c) @ w_ada_final + b_ada_final
        sh, sc = jnp.split(mod, 2, axis=-1)
        return modulate(rmsnorm(x, final_g), sh, sc)

    y_prompt = final_norm(xp, c_prompt)
    y_sample = final_norm(xs, c_sample)
    return (y_prompt, y_sample, jnp.stack(sp_list), jnp.stack(bp_list), jnp.stack(ss_list), jnp.stack(bs_list))
```

```python
import functools

import numpy as np
import jax
import jax.numpy as jnp
from jax import lax
from jax.experimental import pallas as pl
from jax.experimental.pallas import tpu as pltpu

F32 = jnp.float32
BF16 = jnp.bfloat16
EPS = 1e-6
HGRN_HEADS = 8
HEAD_DIM = 128
POOL_WINDOWS = (2, 4, 8, 16)
POOL_GROUP_DIM = 256
POOL_BUF = 15
POOL_HALO = 16
PEER_HEADS = 8
PEER_KEYS = 128
PEER_TOPK = 16
V7X_VMEM_LIMIT_BYTES = 56 * 1024 * 1024
HI = lax.Precision.HIGHEST


def _params(*sem):
    return pltpu.CompilerParams(dimension_semantics=sem, vmem_limit_bytes=V7X_VMEM_LIMIT_BYTES)


def _silu(x):
    return x * jax.nn.sigmoid(x)


def _rmsnorm(x, g):
    return x * lax.rsqrt(jnp.mean(x * x, axis=-1, keepdims=True) + EPS) * g


def _ada_kernel(c_ref, w_ref, b_ref, o_ref):
    a = _silu(c_ref[...]).astype(BF16)
    o_ref[...] = jnp.dot(a, w_ref[...].astype(BF16), preferred_element_type=F32) + b_ref[...]


def ada_mod(c, w, b, tn=1024):
    R, D = c.shape
    M = w.shape[1]
    return pl.pallas_call(
        _ada_kernel,
        grid=(M // tn,),
        in_specs=[pl.BlockSpec((R, D), lambda j: (0, 0)),
                  pl.BlockSpec((D, tn), lambda j: (0, j)),
                  pl.BlockSpec((1, tn), lambda j: (0, j))],
        out_specs=pl.BlockSpec((R, tn), lambda j: (0, j)),
        out_shape=jax.ShapeDtypeStruct((R, M), F32),
        compiler_params=_params("arbitrary"),
        name="ada_mod",
    )(c, w, b.reshape(1, M))


def _nmm_kernel(x_ref, g_ref, sh_ref, sc_ref, w_ref, o_ref, h_scr):
    @pl.when(pl.program_id(1) == 0)
    def _():
        y = _rmsnorm(x_ref[...], g_ref[...])
        h_scr[...] = (y * (1.0 + sc_ref[0]) + sh_ref[0]).astype(BF16)

    o_ref[...] = jnp.dot(h_scr[...], w_ref[...], preferred_element_type=F32)


def norm_mod_matmul(x, g, sh, sc, w, tpg, tm, tn):
    N, D = x.shape
    M = w.shape[1]
    R = sh.shape[1]
    grp = lambda i, j: ((i * tm) // tpg, 0, 0)
    return pl.pallas_call(
        _nmm_kernel,
        grid=(N // tm, M // tn),
        in_specs=[pl.BlockSpec((tm, D), lambda i, j: (i, 0)),
                  pl.BlockSpec((1, D), lambda i, j: (0, 0)),
                  pl.BlockSpec((1, R, D), grp),
                  pl.BlockSpec((1, R, D), grp),
                  pl.BlockSpec((D, tn), lambda i, j: (0, j))],
        out_specs=pl.BlockSpec((tm, tn), lambda i, j: (i, j)),
        out_shape=jax.ShapeDtypeStruct((N, M), F32),
        scratch_shapes=[pltpu.VMEM((tm, D), BF16)],
        compiler_params=_params("parallel", "arbitrary"),
        name="norm_mod_matmul",
    )(x, g.reshape(1, D), sh, sc, w)


def _hgrn_consts(C):
    nlev = int(np.log2(C))
    assert 2 ** nlev == C
    t = np.arange(C)
    L = (t[:, None] >= t[None, :]).astype(np.float32)
    blocks = [L, (t[None, :] > t[:, None]).astype(np.float32)]
    masks = []
    for l in range(nlev):
        bsz = C >> l
        mid = (t // bsz) * bsz + bsz // 2
        blocks.append(L - L[mid - 1])
        same = (t[:, None] // bsz) == (t[None, :] // bsz)
        masks.append((same & ((t % bsz) >= bsz // 2)[:, None] & ((t % bsz) < bsz // 2)[None, :]).astype(np.float32))
    masks.append(np.eye(C, dtype=np.float32))
    return np.concatenate(blocks, 0), np.stack(masks), nlev


def _hgrn_kernel(q_ref, f_ref, i_ref, g_ref, wcat_ref, mask_ref, lbp_ref, gn_ref, s0_ref, o_ref, s_ref,
                 *, bb, C, nlev, t_valid):
    @pl.when(pl.program_id(1) == 0)
    def _():
        s_ref[...] = s0_ref[...]

    wcat = wcat_ref[...]
    ones = jnp.ones((C, HEAD_DIM), F32)
    nt = (((1,), (1,)), ((), ()))
    tn = (((0,), (0,)), ((), ()))
    if t_valid < C:
        valid = lax.broadcasted_iota(jnp.int32, (C, HEAD_DIM), 0) < t_valid

    def per_batch(b, carry):
        for h in range(HGRN_HEADS):
            sl = slice(h * HEAD_DIM, (h + 1) * HEAD_DIM)
            zq, zf, v, zg = q_ref[b, :, sl], f_ref[b, :, sl], i_ref[b, :, sl], g_ref[b, :, sl]
            log_lb, log1m_lb, one_m_lb = lbp_ref[0:1, sl], lbp_ref[1:2, sl], lbp_ref[2:3, sl]
            q = _silu(zq)
            log_sig = jnp.minimum(zf, 0.0) - jnp.log1p(jnp.exp(-jnp.abs(zf)))
            bterm = log1m_lb + log_sig
            logf = jnp.maximum(log_lb, bterm) + jnp.log1p(jnp.exp(-jnp.abs(log_lb - bterm)))
            kk = one_m_lb * jax.nn.sigmoid(-zf)
            if t_valid < C:
                logf = jnp.where(valid, logf, 0.0)
                kk = jnp.where(valid, kk, 0.0)
            dall = jnp.dot(wcat, logf, precision=HI, preferred_element_type=F32)
            G, dk = dall[0:C], dall[C:2 * C]
            S = s_ref[b, h]
            o = jnp.dot(q * jnp.exp(G), S, precision=HI, preferred_element_type=F32)
            A = mask_ref[nlev] * lax.dot_general(q, kk, nt, precision=HI, preferred_element_type=F32)
            for l in range(nlev):
                dl = dall[(2 + l) * C:(3 + l) * C]
                qe = q * jnp.exp(jnp.minimum(dl, 0.0))
                ke = kk * jnp.exp(jnp.minimum(-dl, 0.0))
                A = A + mask_ref[l] * lax.dot_general(qe, ke, nt, precision=HI, preferred_element_type=F32)
            o = o + jnp.dot(A, v, precision=HI, preferred_element_type=F32)
            gfull = lax.dot_general(logf, ones, tn, precision=HI, preferred_element_type=F32)
            kdec = kk * jnp.exp(dk)
            s_ref[b, h] = S * jnp.exp(gfull) + lax.dot_general(kdec, v, tn, precision=HI,
                                                               preferred_element_type=F32)
            o_ref[b, :, sl] = _rmsnorm(o, gn_ref[...]) * _silu(zg)
        return carry

    lax.fori_loop(0, bb, per_batch, 0)


def hgrn_mixer(z, s0, lbp, gn, *, bb, C, t_valid):
    B, T, _ = z.shape
    W = HGRN_HEADS * HEAD_DIM
    wcat, masks, nlev = _hgrn_consts(C)
    zspec = lambda part: pl.BlockSpec((bb, C, W), lambda b, c, part=part: (b, c, part))
    const = lambda a: pl.BlockSpec(a.shape, lambda b, c: (0,) * a.ndim)
    sspec = pl.BlockSpec((bb, HGRN_HEADS, HEAD_DIM, HEAD_DIM), lambda b, c: (b, 0, 0, 0))
    gn2 = gn.reshape(1, HEAD_DIM)
    return pl.pallas_call(
        functools.partial(_hgrn_kernel, bb=bb, C=C, nlev=nlev, t_valid=t_valid),
        grid=(B // bb, T // C),
        in_specs=[zspec(0), zspec(1), zspec(2), zspec(3), const(wcat), const(masks), const(lbp), const(gn2), sspec],
        out_specs=[pl.BlockSpec((bb, C, W), lambda b, c: (b, c, 0)), sspec],
        out_shape=[jax.ShapeDtypeStruct((B, T, W), F32), jax.ShapeDtypeStruct(s0.shape, F32)],
        compiler_params=_params("parallel", "arbitrary"),
        name="hgrn_mixer",
    )(z, z, z, z, jnp.asarray(wcat), jnp.asarray(masks), lbp, gn2, s0)


def _pool_linear(pooled, g, w_ref, b_ref, s_ref):
    sl = slice(g * POOL_GROUP_DIM, (g + 1) * POOL_GROUP_DIM)
    y = jnp.dot(pooled.astype(BF16), w_ref[g], preferred_element_type=F32) + b_ref[:, sl]
    return y * s_ref[:, sl]


def _pool_seq_kernel(cur_ref, halo_ref, w_ref, b_ref, s_ref, o_ref, *, tm, blocks_per_seq):
    blk = pl.program_id(0) % blocks_per_seq
    halo = jnp.where(blk == 0, 0.0, halo_ref[...])
    cur = cur_ref[...]
    full = jnp.concatenate([halo, cur], axis=0)
    pos = blk * tm + lax.broadcasted_iota(jnp.int32, (tm, 1), 0)
    for g, w in enumerate(POOL_WINDOWS):
        sl = slice(g * POOL_GROUP_DIM, (g + 1) * POOL_GROUP_DIM)
        acc = full[:, sl]
        step = 1
        while step < w:
            acc = acc + pltpu.roll(acc, step, axis=0)
            step *= 2
        cnt = jnp.minimum(w, pos + 1).astype(F32)
        pooled = acc[POOL_HALO:] / cnt - cur[:, sl]
        o_ref[:, sl] = _pool_linear(pooled, g, w_ref, b_ref, s_ref)


def pool_mixer_seq(z, T, pw, pb, ps, tm):
    N = z.shape[0]
    W = len(POOL_WINDOWS) * POOL_GROUP_DIM
    col = z.shape[1] // W - 1
    const = lambda a: pl.BlockSpec(a.shape, lambda i: (0,) * a.ndim)
    return pl.pallas_call(
        functools.partial(_pool_seq_kernel, tm=tm, blocks_per_seq=T // tm),
        grid=(N // tm,),
        in_specs=[pl.BlockSpec((tm, W), lambda i: (i, col)),
                  pl.BlockSpec((POOL_HALO, W), lambda i: (jnp.maximum(i * (tm // POOL_HALO) - 1, 0), col)),
                  const(pw), const(pb), const(ps)],
        out_specs=pl.BlockSpec((tm, W), lambda i: (i, 0)),
        out_shape=jax.ShapeDtypeStruct((N, W), F32),
        compiler_params=_params("parallel"),
        name="pool_mixer_seq",
    )(z, z, pw, pb, ps)


def _pool_step_kernel(full_ref, w_ref, b_ref, s_ref, o_ref, *, T, start_pos):
    for t in range(T):
        for g, w in enumerate(POOL_WINDOWS):
            sl = slice(g * POOL_GROUP_DIM, (g + 1) * POOL_GROUP_DIM)
            cur = full_ref[POOL_BUF + t, :, sl]
            acc = cur
            for j in range(1, w):
                acc = acc + full_ref[POOL_BUF + t - j, :, sl]
            pooled = acc / float(min(w, start_pos + t + 1)) - cur
            o_ref[t, :, sl] = _pool_linear(pooled, g, w_ref, b_ref, s_ref)


def pool_mixer_step(full_t, T, start_pos, pw, pb, ps, bb):
    R, B, W = full_t.shape
    const = lambda a: pl.BlockSpec(a.shape, lambda i: (0,) * a.ndim)
    return pl.pallas_call(
        functools.partial(_pool_step_kernel, T=T, start_pos=start_pos),
        grid=(B // bb,),
        in_specs=[pl.BlockSpec((R, bb, W), lambda i: (0, i, 0)), const(pw), const(pb), const(ps)],
        out_specs=pl.BlockSpec((T, bb, W), lambda i: (0, i, 0)),
        out_shape=jax.ShapeDtypeStruct((T, B, W), F32),
        compiler_params=_params("parallel"),
        name="pool_mixer_step",
    )(full_t, pw, pb, ps)


def _oproj_kernel(x_ref, oa_ref, ob_ref, w_ref, g_ref, o_ref):
    wa = w_ref.shape[0] // 2
    mix = jnp.dot(oa_ref[...].astype(BF16), w_ref[0:wa, :], preferred_element_type=F32)
    mix = mix + jnp.dot(ob_ref[...].astype(BF16), w_ref[wa:, :], preferred_element_type=F32)
    o_ref[...] = x_ref[...] + g_ref[0] * mix


def out_proj_residual(x, oa, ob, w, gate, tpg, tm, tn):
    N, D = x.shape
    Wa = oa.shape[1]
    R = gate.shape[1]
    return pl.pallas_call(
        _oproj_kernel,
        grid=(N // tm, D // tn),
        in_specs=[pl.BlockSpec((tm, tn), lambda i, j: (i, j)),
                  pl.BlockSpec((tm, Wa), lambda i, j: (i, 0)),
                  pl.BlockSpec((tm, Wa), lambda i, j: (i, 0)),
                  pl.BlockSpec((2 * Wa, tn), lambda i, j: (0, j)),
                  pl.BlockSpec((1, R, tn), lambda i, j: ((i * tm) // tpg, 0, j))],
        out_specs=pl.BlockSpec((tm, tn), lambda i, j: (i, j)),
        out_shape=jax.ShapeDtypeStruct((N, D), F32),
        compiler_params=_params("parallel", "arbitrary"),
        name="out_proj_residual",
    )(x, oa, ob, w, gate)


def _peer_query_kernel(x_ref, g_ref, sh_ref, sc_ref, wqT_ref, hT_ref, qnT_ref, hT_scr):
    @pl.when(pl.program_id(1) == 0)
    def _():
        y = _rmsnorm(x_ref[...], g_ref[...])
        h = y * (1.0 + sc_ref[0]) + sh_ref[0]
        hT_scr[...] = h.T.astype(BF16)
        hT_ref[...] = hT_scr[...]

    q = jnp.dot(wqT_ref[...], hT_scr[...], preferred_element_type=F32)
    rows, tm = q.shape
    q3 = q.reshape(rows // PEER_KEYS, PEER_KEYS, tm)
    qn = q3 * lax.rsqrt(jnp.mean(q3 * q3, axis=1, keepdims=True) + EPS)
    qnT_ref[...] = qn.reshape(rows, tm).astype(BF16)


def peer_query(x, g, sh, sc, wqT, tpg, tm, tr=256):
    N, D = x.shape
    Q = wqT.shape[0]
    R = sh.shape[1]
    grp = lambda i, j: ((i * tm) // tpg, 0, 0)
    return pl.pallas_call(
        _peer_query_kernel,
        grid=(N // tm, Q // tr),
        in_specs=[pl.BlockSpec((tm, D), lambda i, j: (i, 0)),
                  pl.BlockSpec((1, D), lambda i, j: (0, 0)),
                  pl.BlockSpec((1, R, D), grp),
                  pl.BlockSpec((1, R, D), grp),
                  pl.BlockSpec((tr, D), lambda i, j: (j, 0))],
        out_specs=[pl.BlockSpec((D, tm), lambda i, j: (0, i)),
                   pl.BlockSpec((tr, tm), lambda i, j: (j, i))],
        out_shape=[jax.ShapeDtypeStruct((D, N), BF16), jax.ShapeDtypeStruct((Q, N), BF16)],
        scratch_shapes=[pltpu.VMEM((D, tm), BF16)],
        compiler_params=_params("parallel", "arbitrary"),
        name="peer_query",
    )(x, g.reshape(1, D), sh, sc, wqT)


def _matmul_kernel(a_ref, b_ref, o_ref):
    o_ref[...] = jnp.dot(a_ref[...], b_ref[...], preferred_element_type=F32)


def matmul_wT(a, b, tr, tm):
    M, K = a.shape
    N = b.shape[1]
    return pl.pallas_call(
        _matmul_kernel,
        grid=(N // tm, M // tr),
        in_specs=[pl.BlockSpec((tr, K), lambda i, j: (j, 0)),
                  pl.BlockSpec((K, tm), lambda i, j: (0, i))],
        out_specs=pl.BlockSpec((tr, tm), lambda i, j: (j, i)),
        out_shape=jax.ShapeDtypeStruct((M, N), F32),
        compiler_params=_params("parallel", "arbitrary"),
        name="peer_scores",
    )(a, b)


def _extract_topk(cur_ref, pos_ref, val_ref, rounds):
    K = cur_ref.shape[0]
    kio = lax.broadcasted_iota(jnp.int32, cur_ref.shape, 0)
    pos_ref[...] = jnp.full(pos_ref.shape, rounds, jnp.int32)

    def one_round(r, carry):
        cur = cur_ref[...]
        m = jnp.max(cur, axis=0)
        first = jnp.min(jnp.where(cur == m[None], kio, K), axis=0)
        hit = kio == first[None]
        pos_ref[...] = jnp.where(hit, r, pos_ref[...])
        cur_ref[...] = jnp.where(hit, -jnp.inf, cur)
        val_ref[r] = m
        return carry

    lax.fori_loop(0, rounds, one_round, 0)


_CAND = [(i, j) for i in range(PEER_TOPK) for j in range(PEER_TOPK) if (i + 1) * (j + 1) <= PEER_TOPK]


def _peer_select_kernel(s_ref, cnt_ref, e1_ref, rank_ref, e2_ref,
                        cur_scr, pos1_scr, pos2_scr, v1_scr, v2_scr, cand_scr, posc_scr, vc_scr):
    K, H, L = cnt_ref.shape
    s1 = s_ref[0:K * H, :].reshape(K, H, L)
    s2 = s_ref[K * H:, :].reshape(K, H, L)
    cur_scr[...] = s1
    _extract_topk(cur_scr, pos1_scr, v1_scr, PEER_TOPK)
    cur_scr[...] = s2
    _extract_topk(cur_scr, pos2_scr, v2_scr, PEER_TOPK)

    for c, (i, j) in enumerate(_CAND):
        cand_scr[c] = v1_scr[i] + v2_scr[j]
    _extract_topk(cand_scr, posc_scr, vc_scr, PEER_TOPK)

    top = vc_scr[...]
    inv_z = 1.0 / jnp.sum(jnp.exp(top - top[0:1]), axis=0)
    pos1 = pos1_scr[...]
    cnt = jnp.zeros((K, H, L), F32)
    for i in range(PEER_TOPK):
        cs = [c for c, (ci, _) in enumerate(_CAND) if ci == i]
        n_i = sum((posc_scr[c] < PEER_TOPK).astype(F32) for c in cs)
        cnt = cnt + jnp.where(pos1 == i, n_i[None], 0.0)
    cnt_ref[...] = cnt
    e1_ref[...] = jnp.exp(s1 - v1_scr[0:1]) * inv_z[None]
    rank_ref[...] = pos2_scr[...].astype(F32)
    e2_ref[...] = jnp.exp(s2 - v2_scr[0:1])


def peer_select(scoresT, tl=128):
    rows, N = scoresT.shape
    K, H = PEER_KEYS, PEER_HEADS
    out = jax.ShapeDtypeStruct((K, H, N), F32)
    ospec = pl.BlockSpec((K, H, tl), lambda i: (0, 0, i))
    nc = len(_CAND)
    return pl.pallas_call(
        _peer_select_kernel,
        grid=(N // tl,),
        in_specs=[pl.BlockSpec((rows, tl), lambda i: (0, i))],
        out_specs=[ospec] * 4,
        out_shape=[out] * 4,
        scratch_shapes=[pltpu.VMEM((K, H, tl), F32), pltpu.VMEM((K, H, tl), jnp.int32),
                        pltpu.VMEM((K, H, tl), jnp.int32), pltpu.VMEM((PEER_TOPK, H, tl), F32),
                        pltpu.VMEM((PEER_TOPK, H, tl), F32), pltpu.VMEM((nc, H, tl), F32),
                        pltpu.VMEM((nc, H, tl), jnp.int32), pltpu.VMEM((PEER_TOPK, H, tl), F32)],
        compiler_params=_params("parallel"),
        name="peer_select",
    )(scoresT)


def _gelu(x):
    return 0.5 * x * (1.0 + lax.erf(x * np.float32(1.0 / np.sqrt(2.0))))


def _peer_expert_kernel(u_ref, vT_ref, hT_ref, cnt_ref, e1_ref, rank_ref, e2_ref, o_ref, a_scr, *, te):
    j = pl.program_id(1)

    @pl.when(j == 0)
    def _():
        o_ref[...] = jnp.zeros_like(o_ref)

    sT = jnp.dot(u_ref[...], hT_ref[...], preferred_element_type=F32)
    act = _gelu(sT)
    tm = sT.shape[1]
    for al in range(te // PEER_KEYS):
        a = j * (te // PEER_KEYS) + al
        w = jnp.zeros((PEER_KEYS, tm), F32)
        for h in range(PEER_HEADS):
            c = cnt_ref[h, pl.ds(a, 1), :]
            e1 = e1_ref[h, pl.ds(a, 1), :]
            w = w + jnp.where(rank_ref[h] < c, e2_ref[h], 0.0) * e1
        rows = slice(al * PEER_KEYS, (al + 1) * PEER_KEYS)
        a_scr[rows, :] = (act[rows, :] * w).astype(BF16)
    o_ref[...] += jnp.dot(vT_ref[...], a_scr[...], preferred_element_type=F32)


def peer_experts(u, vT, hT, cnt, e1, rank, e2, tm, te):
    E, D = u.shape
    N = hT.shape[1]
    H, K, _ = cnt.shape
    sel = pl.BlockSpec((H, K, tm), lambda i, j: (0, 0, i))
    return pl.pallas_call(
        functools.partial(_peer_expert_kernel, te=te),
        grid=(N // tm, E // te),
        in_specs=[pl.BlockSpec((te, D), lambda i, j: (j, 0)),
                  pl.BlockSpec((D, te), lambda i, j: (0, j)),
                  pl.BlockSpec((D, tm), lambda i, j: (0, i)),
                  sel, sel, sel, sel],
        out_specs=pl.BlockSpec((D, tm), lambda i, j: (0, i)),
        out_shape=jax.ShapeDtypeStruct((D, N), F32),
        scratch_shapes=[pltpu.VMEM((te, tm), BF16)],
        compiler_params=_params("parallel", "arbitrary"),
        name="peer_experts",
    )(u, vT, hT, cnt, e1, rank, e2)


def _peer_residual_kernel(x_ref, pT_ref, g_ref, o_ref):
    o_ref[...] = x_ref[...] + g_ref[0] * pT_ref[...].T


def peer_residual(x, pT, gate, tpg, tm):
    N, D = x.shape
    R = gate.shape[1]
    return pl.pallas_call(
        _peer_residual_kernel,
        grid=(N // tm,),
        in_specs=[pl.BlockSpec((tm, D), lambda i: (i, 0)),
                  pl.BlockSpec((D, tm), lambda i: (0, i)),
                  pl.BlockSpec((1, R, D), lambda i: ((i * tm) // tpg, 0, 0))],
        out_specs=pl.BlockSpec((tm, D), lambda i: (i, 0)),
        out_shape=jax.ShapeDtypeStruct((N, D), F32),
        compiler_params=_params("parallel"),
        name="peer_residual",
    )(x, pT, gate)


def _final_kernel(x_ref, g_ref, sh_ref, sc_ref, o_ref):
    o_ref[...] = _rmsnorm(x_ref[...], g_ref[...]) * (1.0 + sc_ref[0]) + sh_ref[0]


def final_norm(x, g, sh, sc, tpg, tm):
    N, D = x.shape
    R = sh.shape[1]
    grp = lambda i: ((i * tm) // tpg, 0, 0)
    return pl.pallas_call(
        _final_kernel,
        grid=(N // tm,),
        in_specs=[pl.BlockSpec((tm, D), lambda i: (i, 0)), pl.BlockSpec((1, D), lambda i: (0, 0)),
                  pl.BlockSpec((1, R, D), grp), pl.BlockSpec((1, R, D), grp)],
        out_specs=pl.BlockSpec((tm, D), lambda i: (i, 0)),
        out_shape=jax.ShapeDtypeStruct((N, D), F32),
        compiler_params=_params("parallel"),
        name="final_norm",
    )(x, g.reshape(1, D), sh, sc)


def _key_matrix(keys):
    H, P, K, Dh = keys.shape
    m = jnp.einsum('hpkd,hg,pq->pkhgqd', keys, jnp.eye(H, dtype=keys.dtype), jnp.eye(P, dtype=keys.dtype))
    return m.reshape(P * K * H, H * P * Dh)


def _peer_ffn(x, mod, lw, tpg, tm):
    sh2, sc2, g2 = mod
    hT, qnT = peer_query(x, lw["norm2_g"], sh2, sc2, lw["wqT"], tpg, tm)
    scoresT = matmul_wT(lw["kmat"], qnT, 512, tm)
    sel = peer_select(scoresT)
    cnt, e1, rank, e2 = (jnp.transpose(a, (1, 0, 2)) for a in sel)
    pT = peer_experts(lw["u"], lw["vT"], hT, cnt, e1, rank, e2, tm, 1024)
    return peer_residual(x, pT, g2, tpg, tm)


def _stream_mod(mod, n_mod, expand):
    parts = jnp.split(mod, n_mod, axis=-1)
    if expand:
        return [jnp.repeat(p, expand, axis=0)[None] for p in parts]
    return [p[:, None, :] for p in parts]


def kernel(x_prompt, x_sample, c_prompt, c_sample, state_hgrn, state_pool, w_ada, b_ada, norm1_g, norm2_g, w_in, w_out, lb_logits, hgrn_norm_g, pool_w, pool_b, pool_scale, peer_wq, peer_keys, peer_u, peer_v, final_g, w_ada_final, b_ada_final):
    B, T, D = x_prompt.shape
    Bs, Ts, _ = x_sample.shape
    depth = w_ada.shape[0]
    past_len = 16384
    Wh = HGRN_HEADS * HEAD_DIM
    tm = 512

    lb = jnp.cumsum(jax.nn.softmax(lb_logits.astype(F32), axis=0), axis=0)
    lb = lb - lb[0:1]

    c_all = jnp.concatenate([c_prompt, c_sample], axis=0)
    c_all = jnp.pad(c_all, ((0, (-c_all.shape[0]) % 8), (0, 0)))

    xp = x_prompt.reshape(B * T, D)
    xs = x_sample.reshape(Bs * Ts, D)
    Tpad = 8
    zero_state = jnp.zeros((B, HGRN_HEADS, HEAD_DIM, HEAD_DIM), F32)
    sp_list, bp_list, ss_list, bs_list = [], [], [], []

    for l in range(depth):
        mod = ada_mod(c_all, w_ada[l], b_ada[l])
        mod_p = _stream_mod(mod[:B], 6, 0)
        mod_s = _stream_mod(mod[B:B + Bs], 6, Ts)
        lw = {
            "norm2_g": norm2_g[l],
            "wqT": peer_wq[l].T.astype(BF16),
            "kmat": _key_matrix(peer_keys[l]).astype(BF16),
            "u": peer_u[l].astype(BF16),
            "vT": peer_v[l].T.astype(BF16),
        }
        w_in_b = w_in[l].astype(BF16)
        w_out_b = w_out[l].astype(BF16)
        pw = pool_w[l].astype(BF16)
        pb = pool_b[l].reshape(1, -1)
        ps = pool_scale[l].reshape(1, -1)
        lbp = jnp.stack([jnp.log(lb[l]), jnp.log1p(-lb[l]), 1.0 - lb[l]])

        z = norm_mod_matmul(xp, norm1_g[l], mod_p[0], mod_p[1], w_in_b, T, tm, 1024)
        z3 = z.reshape(B, T, -1)
        oa, sp = hgrn_mixer(z3, zero_state, lbp, hgrn_norm_g[l], bb=1, C=64, t_valid=64)
        ob = pool_mixer_seq(z, T, pw, pb, ps, tm)
        xp = out_proj_residual(xp, oa.reshape(B * T, Wh), ob, w_out_b, mod_p[2], T, tm, 1024)
        xp = _peer_ffn(xp, mod_p[3:6], lw, T, tm)
        sp_list.append(sp)
        bp_list.append(z3[:, T - POOL_BUF:, 4 * Wh:])

        ns = Bs * Ts
        z = norm_mod_matmul(xs, norm1_g[l], mod_s[0], mod_s[1], w_in_b, ns, ns, 1024)
        z3 = z.reshape(Bs, Ts, -1)
        zpad = jnp.pad(z3, ((0, 0), (0, Tpad - Ts), (0, 0)))
        oa, ss = hgrn_mixer(zpad, state_hgrn[l], lbp, hgrn_norm_g[l], bb=8, C=Tpad, t_valid=Ts)
        full = jnp.concatenate([state_pool[l], z3[:, :, 4 * Wh:]], axis=1)
        ob = pool_mixer_step(jnp.transpose(full, (1, 0, 2)), Ts, past_len, pw, pb, ps, 32)
        ob = jnp.transpose(ob, (1, 0, 2)).reshape(ns, -1)
        xs = out_proj_residual(xs, oa[:, :Ts].reshape(ns, Wh), ob, w_out_b, mod_s[2], ns, ns, 1024)
        xs = _peer_ffn(xs, mod_s[3:6], lw, ns, ns)
        ss_list.append(ss)
        bs_list.append(full[:, Ts:])

    modf = ada_mod(c_all, w_ada_final, b_ada_final)
    mf_p = _stream_mod(modf[:B], 2, 0)
    mf_s = _stream_mod(modf[B:B + Bs], 2, Ts)
    y_prompt = final_norm(xp, final_g, mf_p[0], mf_p[1], T, tm).reshape(B, T, D)
    y_sample = final_norm(xs, final_g, mf_s[0], mf_s[1], Bs * Ts, Bs * Ts).reshape(Bs, Ts, D)
    return (y_prompt, y_sample, jnp.stack(sp_list), jnp.stack(bp_list), jnp.stack(ss_list), jnp.stack(bs_list))
```

```python
import functools

import numpy as np
import jax
import jax.numpy as jnp
from jax import lax
from jax.experimental import pallas as pl
from jax.experimental.pallas import tpu as pltpu

F32 = jnp.float32
BF16 = jnp.bfloat16
EPS = 1e-6
HGRN_HEADS = 8
HEAD_DIM = 128
POOL_WINDOWS = (2, 4, 8, 16)
POOL_GROUP_DIM = 256
POOL_BUF = 15
POOL_HALO = 16
PEER_HEADS = 8
PEER_KEYS = 128
PEER_TOPK = 16
PAST_LEN = 16384
V7X_VMEM_LIMIT_BYTES = 56 * 1024 * 1024


def _params(*sem):
    return pltpu.CompilerParams(dimension_semantics=sem, vmem_limit_bytes=V7X_VMEM_LIMIT_BYTES)


def _silu(x):
    return x * jax.nn.sigmoid(x)


def _rmsnorm(x, g):
    return x * lax.rsqrt(jnp.mean(x * x, axis=-1, keepdims=True) + EPS) * g


def _ada_kernel(c_ref, w_ref, b_ref, o_ref):
    a = _silu(c_ref[...]).astype(BF16)
    o_ref[...] = jnp.dot(a, w_ref[...].astype(BF16), preferred_element_type=F32) + b_ref[...]


def ada_mod(c, w, layer, b, tn=1024):
    R, D = c.shape
    M = w.shape[2]
    return pl.pallas_call(
        _ada_kernel,
        grid=(M // tn,),
        in_specs=[pl.BlockSpec((R, D), lambda j: (0, 0)),
                  pl.BlockSpec((None, D, tn), lambda j: (layer, 0, j)),
                  pl.BlockSpec((1, tn), lambda j: (0, j))],
        out_specs=pl.BlockSpec((R, tn), lambda j: (0, j)),
        out_shape=jax.ShapeDtypeStruct((R, M), F32),
        compiler_params=_params("arbitrary"),
        name="ada_mod",
    )(c, w, b.reshape(1, M))


def _nmm_kernel(x_ref, g_ref, sh_ref, sc_ref, w_ref, o_ref, h_scr):
    @pl.when(pl.program_id(1) == 0)
    def _():
        y = _rmsnorm(x_ref[...], g_ref[...])
        h_scr[...] = (y * (1.0 + sc_ref[0]) + sh_ref[0]).astype(BF16)

    o_ref[...] = jnp.dot(h_scr[...], w_ref[...], preferred_element_type=F32)


def norm_mod_matmul(x, g, sh, sc, w, tpg, tm, tn):
    N, D = x.shape
    M = w.shape[1]
    R = sh.shape[1]
    grp = lambda i, j: ((i * tm) // tpg, 0, 0)
    return pl.pallas_call(
        _nmm_kernel,
        grid=(N // tm, M // tn),
        in_specs=[pl.BlockSpec((tm, D), lambda i, j: (i, 0)),
                  pl.BlockSpec((1, D), lambda i, j: (0, 0)),
                  pl.BlockSpec((1, R, D), grp),
                  pl.BlockSpec((1, R, D), grp),
                  pl.BlockSpec((D, tn), lambda i, j: (0, j))],
        out_specs=pl.BlockSpec((tm, tn), lambda i, j: (i, j)),
        out_shape=jax.ShapeDtypeStruct((N, M), F32),
        scratch_shapes=[pltpu.VMEM((tm, D), BF16)],
        compiler_params=_params("parallel", "arbitrary"),
        name="norm_mod_matmul",
    )(x, g.reshape(1, D), sh, sc, w)


def _hgrn_consts(C):
    nlev = int(np.log2(C))
    assert 2 ** nlev == C
    t = np.arange(C)
    L = (t[:, None] >= t[None, :]).astype(np.float32)
    blocks = [L, (t[None, :] > t[:, None]).astype(np.float32)]
    masks = []
    for l in range(nlev):
        bsz = C >> l
        mid = (t // bsz) * bsz + bsz // 2
        blocks.append(L - L[mid - 1])
        same = (t[:, None] // bsz) == (t[None, :] // bsz)
        masks.append((same & ((t % bsz) >= bsz // 2)[:, None] & ((t % bsz) < bsz // 2)[None, :]).astype(np.float32))
    masks.append(np.eye(C, dtype=np.float32))
    return np.concatenate(blocks, 0), np.stack(masks), nlev


def _hgrn_kernel(q_ref, f_ref, i_ref, g_ref, wcat_ref, mask_ref, lbp_ref, gn_ref, s0_ref, o_ref, s_ref,
                 *, bb, C, nlev, t_valid):
    @pl.when(pl.program_id(1) == 0)
    def _():
        s_ref[...] = s0_ref[...]

    wcat = wcat_ref[...].astype(BF16)
    ones = jnp.ones((C, HEAD_DIM), BF16)
    nt = (((1,), (1,)), ((), ()))
    tn = (((0,), (0,)), ((), ()))
    if t_valid < C:
        valid = lax.broadcasted_iota(jnp.int32, (C, HEAD_DIM), 0) < t_valid

    def dot(a, b, dims=None):
        a, b = a.astype(BF16), b.astype(BF16)
        if dims is None:
            return jnp.dot(a, b, preferred_element_type=F32)
        return lax.dot_general(a, b, dims, preferred_element_type=F32)

    def fold3(x):
        return x[..., 0:HEAD_DIM] + x[..., HEAD_DIM:2 * HEAD_DIM] + x[..., 2 * HEAD_DIM:]

    def per_batch(b, carry):
        for h in range(HGRN_HEADS):
            sl = slice(h * HEAD_DIM, (h + 1) * HEAD_DIM)
            zq, zf, v, zg = q_ref[b, :, sl], f_ref[b, :, sl], i_ref[b, :, sl], g_ref[b, :, sl]
            log_lb, log1m_lb, one_m_lb = lbp_ref[0:1, sl], lbp_ref[1:2, sl], lbp_ref[2:3, sl]
            q = _silu(zq)
            log_sig = jnp.minimum(zf, 0.0) - jnp.log1p(jnp.exp(-jnp.abs(zf)))
            bterm = log1m_lb + log_sig
            logf = jnp.maximum(log_lb, bterm) + jnp.log1p(jnp.exp(-jnp.abs(log_lb - bterm)))
            kk = one_m_lb * jax.nn.sigmoid(-zf)
            if t_valid < C:
                logf = jnp.where(valid, logf, 0.0)
                kk = jnp.where(valid, kk, 0.0)
            hi = logf.astype(BF16)
            r1 = logf - hi.astype(F32)
            mid = r1.astype(BF16)
            logf3 = jnp.concatenate([hi, mid, (r1 - mid.astype(F32)).astype(BF16)], axis=1)
            dall = fold3(dot(wcat, logf3))
            G, dk = dall[0:C], dall[C:2 * C]
            S = s_ref[b, h]
            o = dot(q * jnp.exp(G), S)
            A = mask_ref[nlev] * dot(q, kk, nt)
            for l in range(nlev):
                dl = dall[(2 + l) * C:(3 + l) * C]
                qe = q * jnp.exp(jnp.minimum(dl, 0.0))
                ke = kk * jnp.exp(jnp.minimum(-dl, 0.0))
                A = A + mask_ref[l] * dot(qe, ke, nt)
            o = o + dot(A, v)
            g3 = dot(logf3, ones, tn)
            gfull = g3[0:HEAD_DIM] + g3[HEAD_DIM:2 * HEAD_DIM] + g3[2 * HEAD_DIM:]
            kdec = kk * jnp.exp(dk)
            s_ref[b, h] = S * jnp.exp(gfull) + dot(kdec, v, tn)
            o_ref[b, :, sl] = _rmsnorm(o, gn_ref[...]) * _silu(zg)
        return carry

    lax.fori_loop(0, bb, per_batch, 0)


def hgrn_mixer(z, s0, layer, lbp, gn, *, bb, C, t_valid):
    B, T, _ = z.shape
    W = HGRN_HEADS * HEAD_DIM
    wcat, masks, nlev = _hgrn_consts(C)
    zspec = lambda part: pl.BlockSpec((bb, C, W), lambda b, c, part=part: (b, c, part))
    const = lambda a: pl.BlockSpec(a.shape, lambda b, c: (0,) * a.ndim)
    sblock = (bb, HGRN_HEADS, HEAD_DIM, HEAD_DIM)
    sspec = pl.BlockSpec(sblock, lambda b, c: (b, 0, 0, 0))
    s0spec = pl.BlockSpec((None,) + sblock, lambda b, c: (layer, b, 0, 0, 0))
    gn2 = gn.reshape(1, HEAD_DIM)
    return pl.pallas_call(
        functools.partial(_hgrn_kernel, bb=bb, C=C, nlev=nlev, t_valid=t_valid),
        grid=(B // bb, T // C),
        in_specs=[zspec(0), zspec(1), zspec(2), zspec(3), const(wcat), const(masks), const(lbp), const(gn2), s0spec],
        out_specs=[pl.BlockSpec((bb, C, W), lambda b, c: (b, c, 0)), sspec],
        out_shape=[jax.ShapeDtypeStruct((B, T, W), F32), jax.ShapeDtypeStruct(s0.shape[1:], F32)],
        compiler_params=_params("parallel", "arbitrary"),
        name="hgrn_mixer",
    )(z, z, z, z, jnp.asarray(wcat), jnp.asarray(masks), lbp, gn2, s0)


def _pool_linear(pooled, g, w_ref, b_ref, s_ref):
    sl = slice(g * POOL_GROUP_DIM, (g + 1) * POOL_GROUP_DIM)
    y = jnp.dot(pooled.astype(BF16), w_ref[g], preferred_element_type=F32) + b_ref[:, sl]
    return y * s_ref[:, sl]


def _pool_seq_kernel(cur_ref, halo_ref, w_ref, b_ref, s_ref, o_ref, *, tm, blocks_per_seq):
    blk = pl.program_id(0) % blocks_per_seq
    halo = jnp.where(blk == 0, 0.0, halo_ref[...])
    cur = cur_ref[...]
    full = jnp.concatenate([halo, cur], axis=0)
    pos = blk * tm + lax.broadcasted_iota(jnp.int32, (tm, 1), 0)
    for g, w in enumerate(POOL_WINDOWS):
        sl = slice(g * POOL_GROUP_DIM, (g + 1) * POOL_GROUP_DIM)
        acc = full[:, sl]
        step = 1
        while step < w:
            acc = acc + pltpu.roll(acc, step, axis=0)
            step *= 2
        cnt = jnp.minimum(w, pos + 1).astype(F32)
        pooled = acc[POOL_HALO:] / cnt - cur[:, sl]
        o_ref[:, sl] = _pool_linear(pooled, g, w_ref, b_ref, s_ref)


def pool_mixer_seq(z, T, pw, pb, ps, tm):
    N = z.shape[0]
    W = len(POOL_WINDOWS) * POOL_GROUP_DIM
    col = z.shape[1] // W - 1
    const = lambda a: pl.BlockSpec(a.shape, lambda i: (0,) * a.ndim)
    return pl.pallas_call(
        functools.partial(_pool_seq_kernel, tm=tm, blocks_per_seq=T // tm),
        grid=(N // tm,),
        in_specs=[pl.BlockSpec((tm, W), lambda i: (i, col)),
                  pl.BlockSpec((POOL_HALO, W), lambda i: (jnp.maximum(i * (tm // POOL_HALO) - 1, 0), col)),
                  const(pw), const(pb), const(ps)],
        out_specs=pl.BlockSpec((tm, W), lambda i: (i, 0)),
        out_shape=jax.ShapeDtypeStruct((N, W), F32),
        compiler_params=_params("parallel"),
        name="pool_mixer_seq",
    )(z, z, pw, pb, ps)


def _pool_step_kernel(full_ref, w_ref, b_ref, s_ref, o_ref, *, T, start_pos):
    for t in range(T):
        for g, w in enumerate(POOL_WINDOWS):
            sl = slice(g * POOL_GROUP_DIM, (g + 1) * POOL_GROUP_DIM)
            cur = full_ref[POOL_BUF + t, :, sl]
            acc = cur
            for j in range(1, w):
                acc = acc + full_ref[POOL_BUF + t - j, :, sl]
            pooled = acc / float(min(w, start_pos + t + 1)) - cur
            o_ref[t, :, sl] = _pool_linear(pooled, g, w_ref, b_ref, s_ref)


def pool_mixer_step(full_t, T, start_pos, pw, pb, ps, bb):
    R, B, W = full_t.shape
    const = lambda a: pl.BlockSpec(a.shape, lambda i: (0,) * a.ndim)
    return pl.pallas_call(
        functools.partial(_pool_step_kernel, T=T, start_pos=start_pos),
        grid=(B // bb,),
        in_specs=[pl.BlockSpec((R, bb, W), lambda i: (0, i, 0)), const(pw), const(pb), const(ps)],
        out_specs=pl.BlockSpec((T, bb, W), lambda i: (0, i, 0)),
        out_shape=jax.ShapeDtypeStruct((T, B, W), F32),
        compiler_params=_params("parallel"),
        name="pool_mixer_step",
    )(full_t, pw, pb, ps)


def _oproj_kernel(x_ref, oa_ref, ob_ref, w_ref, g_ref, o_ref):
    wa = w_ref.shape[0] // 2
    mix = jnp.dot(oa_ref[...].astype(BF16), w_ref[0:wa, :], preferred_element_type=F32)
    mix = mix + jnp.dot(ob_ref[...].astype(BF16), w_ref[wa:, :], preferred_element_type=F32)
    o_ref[...] = x_ref[...] + g_ref[0] * mix


def out_proj_residual(x, oa, ob, w, gate, tpg, tm, tn):
    N, D = x.shape
    Wa = oa.shape[1]
    R = gate.shape[1]
    return pl.pallas_call(
        _oproj_kernel,
        grid=(N // tm, D // tn),
        in_specs=[pl.BlockSpec((tm, tn), lambda i, j: (i, j)),
                  pl.BlockSpec((tm, Wa), lambda i, j: (i, 0)),
                  pl.BlockSpec((tm, Wa), lambda i, j: (i, 0)),
                  pl.BlockSpec((2 * Wa, tn), lambda i, j: (0, j)),
                  pl.BlockSpec((1, R, tn), lambda i, j: ((i * tm) // tpg, 0, j))],
        out_specs=pl.BlockSpec((tm, tn), lambda i, j: (i, j)),
        out_shape=jax.ShapeDtypeStruct((N, D), F32),
        compiler_params=_params("parallel", "arbitrary"),
        name="out_proj_residual",
    )(x, oa, ob, w, gate)


def _peer_query_kernel(x_ref, g_ref, sh_ref, sc_ref, wqT_ref, hT_ref, qnT_ref, hT_scr):
    @pl.when(pl.program_id(1) == 0)
    def _():
        y = _rmsnorm(x_ref[...], g_ref[...])
        h = y * (1.0 + sc_ref[0]) + sh_ref[0]
        hT_scr[...] = h.T.astype(BF16)
        hT_ref[...] = hT_scr[...]

    q = jnp.dot(wqT_ref[...], hT_scr[...], preferred_element_type=F32)
    rows, tm = q.shape
    q3 = q.reshape(rows // PEER_KEYS, PEER_KEYS, tm)
    qn = q3 * lax.rsqrt(jnp.mean(q3 * q3, axis=1, keepdims=True) + EPS)
    qnT_ref[...] = qn.reshape(rows, tm).astype(BF16)


def peer_query(x, g, sh, sc, wqT, tpg, tm, tr=256):
    N, D = x.shape
    Q = wqT.shape[0]
    R = sh.shape[1]
    grp = lambda i, j: ((i * tm) // tpg, 0, 0)
    return pl.pallas_call(
        _peer_query_kernel,
        grid=(N // tm, Q // tr),
        in_specs=[pl.BlockSpec((tm, D), lambda i, j: (i, 0)),
                  pl.BlockSpec((1, D), lambda i, j: (0, 0)),
                  pl.BlockSpec((1, R, D), grp),
                  pl.BlockSpec((1, R, D), grp),
                  pl.BlockSpec((tr, D), lambda i, j: (j, 0))],
        out_specs=[pl.BlockSpec((D, tm), lambda i, j: (0, i)),
                   pl.BlockSpec((tr, tm), lambda i, j: (j, i))],
        out_shape=[jax.ShapeDtypeStruct((D, N), BF16), jax.ShapeDtypeStruct((Q, N), BF16)],
        scratch_shapes=[pltpu.VMEM((D, tm), BF16)],
        compiler_params=_params("parallel", "arbitrary"),
        name="peer_query",
    )(x, g.reshape(1, D), sh, sc, wqT)


def _matmul_kernel(a_ref, b_ref, o_ref):
    o_ref[...] = jnp.dot(a_ref[...], b_ref[...], preferred_element_type=F32)


def matmul_wT(a, b, tr, tm):
    M, K = a.shape
    N = b.shape[1]
    return pl.pallas_call(
        _matmul_kernel,
        grid=(N // tm, M // tr),
        in_specs=[pl.BlockSpec((tr, K), lambda i, j: (j, 0)),
                  pl.BlockSpec((K, tm), lambda i, j: (0, i))],
        out_specs=pl.BlockSpec((tr, tm), lambda i, j: (j, i)),
        out_shape=jax.ShapeDtypeStruct((M, N), F32),
        compiler_params=_params("parallel", "arbitrary"),
        name="peer_scores",
    )(a, b)


def _extract_topk(cur_ref, pos_ref, val_ref, rounds):
    K = cur_ref.shape[0]
    kio = lax.broadcasted_iota(jnp.int32, cur_ref.shape, 0)
    pos_ref[...] = jnp.full(pos_ref.shape, rounds, jnp.int32)

    def one_round(r, carry):
        cur = cur_ref[...]
        m = jnp.max(cur, axis=0)
        first = jnp.min(jnp.where(cur == m[None], kio, K), axis=0)
        hit = kio == first[None]
        pos_ref[...] = jnp.where(hit, r, pos_ref[...])
        cur_ref[...] = jnp.where(hit, -jnp.inf, cur)
        val_ref[r] = m
        return carry

    lax.fori_loop(0, rounds, one_round, 0)


_CAND = [(i, j) for i in range(PEER_TOPK) for j in range(PEER_TOPK) if (i + 1) * (j + 1) <= PEER_TOPK]


def _peer_select_kernel(s_ref, cnt_ref, e1_ref, rank_ref, e2_ref,
                        cur_scr, pos1_scr, pos2_scr, v1_scr, v2_scr, cand_scr, posc_scr, vc_scr):
    K, H, L = cnt_ref.shape
    s1 = s_ref[0:K * H, :].reshape(K, H, L)
    s2 = s_ref[K * H:, :].reshape(K, H, L)
    cur_scr[...] = s1
    _extract_topk(cur_scr, pos1_scr, v1_scr, PEER_TOPK)
    cur_scr[...] = s2
    _extract_topk(cur_scr, pos2_scr, v2_scr, PEER_TOPK)

    for c, (i, j) in enumerate(_CAND):
        cand_scr[c] = v1_scr[i] + v2_scr[j]
    _extract_topk(cand_scr, posc_scr, vc_scr, PEER_TOPK)

    top = vc_scr[...]
    inv_z = 1.0 / jnp.sum(jnp.exp(top - top[0:1]), axis=0)
    pos1 = pos1_scr[...]
    cnt = jnp.zeros((K, H, L), F32)
    for i in range(PEER_TOPK):
        cs = [c for c, (ci, _) in enumerate(_CAND) if ci == i]
        n_i = sum((posc_scr[c] < PEER_TOPK).astype(F32) for c in cs)
        cnt = cnt + jnp.where(pos1 == i, n_i[None], 0.0)
    cnt_ref[...] = cnt
    e1_ref[...] = jnp.exp(s1 - v1_scr[0:1]) * inv_z[None]
    rank_ref[...] = pos2_scr[...].astype(F32)
    e2_ref[...] = jnp.exp(s2 - v2_scr[0:1])


def peer_select(scoresT, tl=128):
    rows, N = scoresT.shape
    K, H = PEER_KEYS, PEER_HEADS
    out = jax.ShapeDtypeStruct((K, H, N), F32)
    ospec = pl.BlockSpec((K, H, tl), lambda i: (0, 0, i))
    nc = len(_CAND)
    return pl.pallas_call(
        _peer_select_kernel,
        grid=(N // tl,),
        in_specs=[pl.BlockSpec((rows, tl), lambda i: (0, i))],
        out_specs=[ospec] * 4,
        out_shape=[out] * 4,
        scratch_shapes=[pltpu.VMEM((K, H, tl), F32), pltpu.VMEM((K, H, tl), jnp.int32),
                        pltpu.VMEM((K, H, tl), jnp.int32), pltpu.VMEM((PEER_TOPK, H, tl), F32),
                        pltpu.VMEM((PEER_TOPK, H, tl), F32), pltpu.VMEM((nc, H, tl), F32),
                        pltpu.VMEM((nc, H, tl), jnp.int32), pltpu.VMEM((PEER_TOPK, H, tl), F32)],
        compiler_params=_params("parallel"),
        name="peer_select",
    )(scoresT)


def _gelu(x):
    return 0.5 * x * (1.0 + lax.erf(x * np.float32(1.0 / np.sqrt(2.0))))


def _peer_expert_kernel(u_ref, v_ref, hT_ref, cnt_ref, e1_ref, rank_ref, e2_ref, o_ref, a_scr, *, te, nj):
    s = pl.program_id(0)
    prev = jnp.maximum(s - 1, 0)
    cur_slot = s % 2

    @pl.when((s == 0) | (prev % nj == 0))
    def _():
        o_ref[...] = jnp.zeros_like(o_ref)

    @pl.when(s == 0)
    def _():
        a_scr[1] = jnp.zeros(a_scr.shape[1:], a_scr.dtype)

    sT = jnp.dot(u_ref[...], hT_ref[...], preferred_element_type=F32)
    o_ref[...] += lax.dot_general(a_scr[1 - cur_slot], v_ref[...], (((0,), (0,)), ((), ())),
                                  preferred_element_type=F32)
    act = _gelu(sT).astype(BF16)
    tm = sT.shape[1]
    zero = jnp.zeros((), BF16)
    for al in range(te // PEER_KEYS):
        a = (s % nj) * (te // PEER_KEYS) + al
        w = jnp.zeros((PEER_KEYS, tm), BF16)
        for h in range(PEER_HEADS):
            c = cnt_ref[h, pl.ds(a, 1), :].astype(BF16)
            e1 = e1_ref[h, pl.ds(a, 1), :].astype(BF16)
            w = w + jnp.where(rank_ref[h] < c, e2_ref[h], zero) * e1
        a_scr[cur_slot, pl.ds(al * PEER_KEYS, PEER_KEYS), :] = act[al * PEER_KEYS:(al + 1) * PEER_KEYS, :] * w


def peer_experts(u, v, hT, cnt, e1, rank, e2, tm, te):
    E, D = u.shape
    N = hT.shape[1]
    H, K, _ = cnt.shape
    ni, nj = N // tm, E // te
    tok = lambda s: jnp.minimum(s // nj, ni - 1)
    sel = pl.BlockSpec((H, K, tm), lambda s: (0, 0, tok(s)))
    return pl.pallas_call(
        functools.partial(_peer_expert_kernel, te=te, nj=nj),
        grid=(ni * nj + 1,),
        in_specs=[pl.BlockSpec((te, D), lambda s: (s % nj, 0)),
                  pl.BlockSpec((te, D), lambda s: (jnp.maximum(s - 1, 0) % nj, 0)),
                  pl.BlockSpec((D, tm), lambda s: (0, tok(s))),
                  sel, sel, sel, sel],
        out_specs=pl.BlockSpec((tm, D), lambda s: (jnp.maximum(s - 1, 0) // nj, 0)),
        out_shape=jax.ShapeDtypeStruct((N, D), F32),
        scratch_shapes=[pltpu.VMEM((2, te, tm), BF16)],
        compiler_params=_params("arbitrary"),
        name="peer_experts",
    )(u, v, hT, cnt, e1, rank, e2)


def _peer_residual_kernel(x_ref, p_ref, g_ref, o_ref):
    o_ref[...] = x_ref[...] + g_ref[0] * p_ref[...]


def peer_residual(x, p, gate, tpg, tm):
    N, D = x.shape
    R = gate.shape[1]
    return pl.pallas_call(
        _peer_residual_kernel,
        grid=(N // tm,),
        in_specs=[pl.BlockSpec((tm, D), lambda i: (i, 0)),
                  pl.BlockSpec((tm, D), lambda i: (i, 0)),
                  pl.BlockSpec((1, R, D), lambda i: ((i * tm) // tpg, 0, 0))],
        out_specs=pl.BlockSpec((tm, D), lambda i: (i, 0)),
        out_shape=jax.ShapeDtypeStruct((N, D), F32),
        compiler_params=_params("parallel"),
        name="peer_residual",
    )(x, p, gate)


def _final_kernel(x_ref, g_ref, sh_ref, sc_ref, o_ref):
    o_ref[...] = _rmsnorm(x_ref[...], g_ref[...]) * (1.0 + sc_ref[0]) + sh_ref[0]


def final_norm(x, g, sh, sc, tpg, tm):
    N, D = x.shape
    R = sh.shape[1]
    grp = lambda i: ((i * tm) // tpg, 0, 0)
    return pl.pallas_call(
        _final_kernel,
        grid=(N // tm,),
        in_specs=[pl.BlockSpec((tm, D), lambda i: (i, 0)), pl.BlockSpec((1, D), lambda i: (0, 0)),
                  pl.BlockSpec((1, R, D), grp), pl.BlockSpec((1, R, D), grp)],
        out_specs=pl.BlockSpec((tm, D), lambda i: (i, 0)),
        out_shape=jax.ShapeDtypeStruct((N, D), F32),
        compiler_params=_params("parallel"),
        name="final_norm",
    )(x, g.reshape(1, D), sh, sc)


def _key_matrix(keys):
    H, P, K, Dh = keys.shape
    m = jnp.einsum('hpkd,hg,pq->pkhgqd', keys, jnp.eye(H, dtype=keys.dtype), jnp.eye(P, dtype=keys.dtype))
    return m.reshape(P * K * H, H * P * Dh)


def _peer_ffn(x, mod, lw, tpg, tm):
    sh2, sc2, g2 = mod
    hT, qnT = peer_query(x, lw["norm2_g"], sh2, sc2, lw["wqT"], tpg, tm)
    scoresT = matmul_wT(lw["kmat"], qnT, 512, tm)
    sel = peer_select(scoresT)
    cnt, e1, rank, e2 = (jnp.transpose(a, (1, 0, 2)) for a in sel)
    p = peer_experts(lw["u"], lw["v"], hT, cnt, e1, rank.astype(BF16), e2.astype(BF16), tm, 1024)
    return peer_residual(x, p, g2, tpg, tm)


def _stream_mod(mod, n_mod, expand):
    parts = jnp.split(mod, n_mod, axis=-1)
    if expand:
        return [jnp.repeat(p, expand, axis=0)[None] for p in parts]
    return [p[:, None, :] for p in parts]


def kernel(x_prompt, x_sample, c_prompt, c_sample, state_hgrn, state_pool, w_ada, b_ada, norm1_g, norm2_g, w_in, w_out, lb_logits, hgrn_norm_g, pool_w, pool_b, pool_scale, peer_wq, peer_keys, peer_u, peer_v, final_g, w_ada_final, b_ada_final):
    B, T, D = x_prompt.shape
    Bs, Ts, _ = x_sample.shape
    depth = w_ada.shape[0]
    Wh = HGRN_HEADS * HEAD_DIM
    tm = 512

    lb = jnp.cumsum(jax.nn.softmax(lb_logits.astype(F32), axis=0), axis=0)
    lb = lb - lb[0:1]

    c_all = jnp.concatenate([c_prompt, c_sample], axis=0)
    c_all = jnp.pad(c_all, ((0, (-c_all.shape[0]) % 8), (0, 0)))

    xp = x_prompt.reshape(B * T, D)
    xs = x_sample.reshape(Bs * Ts, D)
    Tpad = 8
    zero_state = jnp.zeros((1, B, HGRN_HEADS, HEAD_DIM, HEAD_DIM), F32)
    sp_list, bp_list, ss_list, bs_list = [], [], [], []

    for l in range(depth):
        mod = ada_mod(c_all, w_ada, l, b_ada[l])
        mod_p = _stream_mod(mod[:B], 6, 0)
        mod_s = _stream_mod(mod[B:B + Bs], 6, Ts)
        lw = {
            "norm2_g": norm2_g[l],
            "wqT": peer_wq[l].T.astype(BF16),
            "kmat": _key_matrix(peer_keys[l]).astype(BF16),
            "u": peer_u[l].astype(BF16),
            "v": peer_v[l].astype(BF16),
        }
        w_in_b = w_in[l].astype(BF16)
        w_out_b = w_out[l].astype(BF16)
        pw = pool_w[l].astype(BF16)
        pb = pool_b[l].reshape(1, -1)
        ps = pool_scale[l].reshape(1, -1)
        lbp = jnp.stack([jnp.log(lb[l]), jnp.log1p(-lb[l]), 1.0 - lb[l]])

        z = norm_mod_matmul(xp, norm1_g[l], mod_p[0], mod_p[1], w_in_b, T, tm, 1024)
        z3 = z.reshape(B, T, -1)
        oa, sp = hgrn_mixer(z3, zero_state, 0, lbp, hgrn_norm_g[l], bb=1, C=64, t_valid=64)
        ob = pool_mixer_seq(z, T, pw, pb, ps, tm)
        xp = out_proj_residual(xp, oa.reshape(B * T, Wh), ob, w_out_b, mod_p[2], T, tm, 1024)
        xp = _peer_ffn(xp, mod_p[3:6], lw, T, tm)
        sp_list.append(sp)
        bp_list.append(z3[:, T - POOL_BUF:, 4 * Wh:])

        ns = Bs * Ts
        z = norm_mod_matmul(xs, norm1_g[l], mod_s[0], mod_s[1], w_in_b, ns, ns, 1024)
        z3 = z.reshape(Bs, Ts, -1)
        zpad = jnp.pad(z3, ((0, 0), (0, Tpad - Ts), (0, 0)))
        oa, ss = hgrn_mixer(zpad, state_hgrn, l, lbp, hgrn_norm_g[l], bb=8, C=Tpad, t_valid=Ts)
        full = jnp.concatenate([state_pool[l], z3[:, :, 4 * Wh:]], axis=1)
        ob = pool_mixer_step(jnp.transpose(full, (1, 0, 2)), Ts, PAST_LEN, pw, pb, ps, 32)
        ob = jnp.transpose(ob, (1, 0, 2)).reshape(ns, -1)
        xs = out_proj_residual(xs, oa[:, :Ts].reshape(ns, Wh), ob, w_out_b, mod_s[2], ns, ns, 1024)
        xs = _peer_ffn(xs, mod_s[3:6], lw, ns, ns)
        ss_list.append(ss)
        bs_list.append(full[:, Ts:])

    modf = ada_mod(c_all, w_ada_final[None], 0, b_ada_final)
    mf_p = _stream_mod(modf[:B], 2, 0)
    mf_s = _stream_mod(modf[B:B + Bs], 2, Ts)
    y_prompt = final_norm(xp, final_g, mf_p[0], mf_p[1], T, tm).reshape(B, T, D)
    y_sample = final_norm(xs, final_g, mf_s[0], mf_s[1], Bs * Ts, Bs * Ts).reshape(Bs, Ts, D)
    return (y_prompt, y_sample, jnp.stack(sp_list), jnp.stack(bp_list), jnp.stack(ss_list), jnp.stack(bs_list))
```

```python
import functools

import numpy as np
import jax
import jax.numpy as jnp
from jax import lax
from jax.experimental import pallas as pl
from jax.experimental.pallas import tpu as pltpu

F32 = jnp.float32
BF16 = jnp.bfloat16
EPS = 1e-6
HGRN_HEADS = 8
HEAD_DIM = 128
POOL_WINDOWS = (2, 4, 8, 16)
POOL_GROUP_DIM = 256
POOL_BUF = 15
POOL_HALO = 16
PEER_HEADS = 8
PEER_KEYS = 128
PEER_TOPK = 16
PAST_LEN = 16384
V7X_VMEM_LIMIT_BYTES = 56 * 1024 * 1024


def _params(*sem):
    return pltpu.CompilerParams(dimension_semantics=sem, vmem_limit_bytes=V7X_VMEM_LIMIT_BYTES)


def _silu(x):
    return x * jax.nn.sigmoid(x)


def _rmsnorm(x, g):
    return x * lax.rsqrt(jnp.mean(x * x, axis=-1, keepdims=True) + EPS) * g


class Stream:
    def __init__(self, tm, n_seq, blocks_per_seq):
        self.tm = tm
        self.n_seq = n_seq
        self.blocks_per_seq = blocks_per_seq
        self.n_prompt_blocks = n_seq * blocks_per_seq
        self.n_blocks = self.n_prompt_blocks + 1

    def seq_of(self, i):
        return jnp.minimum(i // self.blocks_per_seq, self.n_seq - 1)


def _mod_specs(st, width, col=None):
    if col is None:
        return [pl.BlockSpec((1, 1, width), lambda i, *_: (st.seq_of(i), 0, 0)),
                pl.BlockSpec((st.tm, width), lambda i, *_: (0, 0))]
    return [pl.BlockSpec((1, 1, width), lambda i, j: (st.seq_of(i), 0, j)),
            pl.BlockSpec((st.tm, width), lambda i, j: (0, j))]


def _mod(seq_ref, tok_ref, decode):
    return jnp.where(decode, tok_ref[...], seq_ref[0])


def _ada_kernel(c_ref, w_ref, b_ref, o_ref):
    a = _silu(c_ref[...]).astype(BF16)
    o_ref[...] = jnp.dot(a, w_ref[...].astype(BF16), preferred_element_type=F32) + b_ref[...]


def ada_mod(c, w, layer, b, tn=1024):
    R, D = c.shape
    M = w.shape[2]
    return pl.pallas_call(
        _ada_kernel,
        grid=(M // tn,),
        in_specs=[pl.BlockSpec((R, D), lambda j: (0, 0)),
                  pl.BlockSpec((None, D, tn), lambda j: (layer, 0, j)),
                  pl.BlockSpec((1, tn), lambda j: (0, j))],
        out_specs=pl.BlockSpec((R, tn), lambda j: (0, j)),
        out_shape=jax.ShapeDtypeStruct((R, M), F32),
        compiler_params=_params("arbitrary"),
        name="ada_mod",
    )(c, w, b.reshape(1, M))


def _nmm_kernel(x_ref, g_ref, shq_ref, sht_ref, scq_ref, sct_ref, w_ref, o_ref, h_scr, *, n_prompt_blocks):
    @pl.when(pl.program_id(1) == 0)
    def _():
        decode = pl.program_id(0) >= n_prompt_blocks
        y = _rmsnorm(x_ref[...], g_ref[...])
        h_scr[...] = (y * (1.0 + _mod(scq_ref, sct_ref, decode)) + _mod(shq_ref, sht_ref, decode)).astype(BF16)

    o_ref[...] = jnp.dot(h_scr[...], w_ref[...], preferred_element_type=F32)


def norm_mod_matmul(st, x, g, sh, sc, w, layer, tn):
    N, D = x.shape
    M = w.shape[2]
    return pl.pallas_call(
        functools.partial(_nmm_kernel, n_prompt_blocks=st.n_prompt_blocks),
        grid=(st.n_blocks, M // tn),
        in_specs=[pl.BlockSpec((st.tm, D), lambda i, j: (i, 0)),
                  pl.BlockSpec((1, D), lambda i, j: (0, 0)),
                  *_mod_specs(st, D), *_mod_specs(st, D),
                  pl.BlockSpec((None, D, tn), lambda i, j: (layer, 0, j))],
        out_specs=pl.BlockSpec((st.tm, tn), lambda i, j: (i, j)),
        out_shape=jax.ShapeDtypeStruct((N, M), F32),
        scratch_shapes=[pltpu.VMEM((st.tm, D), BF16)],
        compiler_params=_params("parallel", "arbitrary"),
        name="norm_mod_matmul",
    )(x, g.reshape(1, D), *sh, *sc, w)


def _hgrn_consts(C):
    nlev = int(np.log2(C))
    assert 2 ** nlev == C
    t = np.arange(C)
    L = (t[:, None] >= t[None, :]).astype(np.float32)
    blocks = [L, (t[None, :] > t[:, None]).astype(np.float32)]
    masks = []
    for l in range(nlev):
        bsz = C >> l
        mid = (t // bsz) * bsz + bsz // 2
        blocks.append(L - L[mid - 1])
        same = (t[:, None] // bsz) == (t[None, :] // bsz)
        masks.append((same & ((t % bsz) >= bsz // 2)[:, None] & ((t % bsz) < bsz // 2)[None, :]).astype(np.float32))
    masks.append(np.eye(C, dtype=np.float32))
    return np.concatenate(blocks, 0), np.stack(masks), nlev


def _hgrn_kernel(q_ref, f_ref, i_ref, g_ref, wcat_ref, mask_ref, lbp_ref, gn_ref, s0_ref, o_ref, s_ref,
                 *, bb, C, nlev, t_valid):
    @pl.when(pl.program_id(1) == 0)
    def _():
        s_ref[...] = s0_ref[...]

    wcat = wcat_ref[...].astype(BF16)
    ones = jnp.ones((C, HEAD_DIM), BF16)
    nt = (((1,), (1,)), ((), ()))
    tn = (((0,), (0,)), ((), ()))
    if t_valid < C:
        valid = lax.broadcasted_iota(jnp.int32, (C, HEAD_DIM), 0) < t_valid

    def dot(a, b, dims=None):
        a, b = a.astype(BF16), b.astype(BF16)
        if dims is None:
            return jnp.dot(a, b, preferred_element_type=F32)
        return lax.dot_general(a, b, dims, preferred_element_type=F32)

    def fold3(x):
        return x[..., 0:HEAD_DIM] + x[..., HEAD_DIM:2 * HEAD_DIM] + x[..., 2 * HEAD_DIM:]

    def per_batch(b, carry):
        for h in range(HGRN_HEADS):
            sl = slice(h * HEAD_DIM, (h + 1) * HEAD_DIM)
            zq, zf, v, zg = q_ref[b, :, sl], f_ref[b, :, sl], i_ref[b, :, sl], g_ref[b, :, sl]
            log_lb, log1m_lb, one_m_lb = lbp_ref[0:1, sl], lbp_ref[1:2, sl], lbp_ref[2:3, sl]
            q = _silu(zq)
            e = jnp.exp(-jnp.abs(zf))
            r = 1.0 / (1.0 + e)
            log_sig = jnp.minimum(zf, 0.0) - jnp.log(1.0 + e)
            bterm = log1m_lb + log_sig
            logf = jnp.maximum(log_lb, bterm) + jnp.log(1.0 + jnp.exp(-jnp.abs(log_lb - bterm)))
            kk = one_m_lb * jnp.where(zf >= 0.0, e * r, r)
            if t_valid < C:
                logf = jnp.where(valid, logf, 0.0)
                kk = jnp.where(valid, kk, 0.0)
            hi = logf.astype(BF16)
            r1 = logf - hi.astype(F32)
            mid = r1.astype(BF16)
            logf3 = jnp.concatenate([hi, mid, (r1 - mid.astype(F32)).astype(BF16)], axis=1)
            dall = fold3(dot(wcat, logf3))
            G, dk = dall[0:C], dall[C:2 * C]
            S = s_ref[b, h]
            o = dot(q * jnp.exp(G), S)
            A = mask_ref[nlev] * dot(q, kk, nt)
            for l in range(nlev):
                dl = dall[(2 + l) * C:(3 + l) * C]
                qe = q * jnp.exp(jnp.minimum(dl, 0.0))
                ke = kk * jnp.exp(jnp.minimum(-dl, 0.0))
                A = A + mask_ref[l] * dot(qe, ke, nt)
            o = o + dot(A, v)
            g3 = dot(logf3, ones, tn)
            gfull = g3[0:HEAD_DIM] + g3[HEAD_DIM:2 * HEAD_DIM] + g3[2 * HEAD_DIM:]
            kdec = kk * jnp.exp(dk)
            s_ref[b, h] = S * jnp.exp(gfull) + dot(kdec, v, tn)
            o_ref[b, :, sl] = _rmsnorm(o, gn_ref[...]) * _silu(zg)
        return carry

    lax.fori_loop(0, bb, per_batch, 0)


def hgrn_mixer(z, zmap, omap, o_rows, s0, layer, lbp, gn, *, grid, bb, C, t_valid):
    W = HGRN_HEADS * HEAD_DIM
    wcat, masks, nlev = _hgrn_consts(C)
    zspec = lambda part: pl.BlockSpec((bb, C, W), lambda b, c, part=part: (*zmap(b, c), part))
    const = lambda a: pl.BlockSpec(a.shape, lambda b, c: (0,) * a.ndim)
    sblock = (bb, HGRN_HEADS, HEAD_DIM, HEAD_DIM)
    sspec = pl.BlockSpec(sblock, lambda b, c: (b, 0, 0, 0))
    s0spec = pl.BlockSpec((None,) + sblock, lambda b, c: (layer, b, 0, 0, 0))
    gn2 = gn.reshape(1, HEAD_DIM)
    return pl.pallas_call(
        functools.partial(_hgrn_kernel, bb=bb, C=C, nlev=nlev, t_valid=t_valid),
        grid=grid,
        in_specs=[zspec(0), zspec(1), zspec(2), zspec(3), const(wcat), const(masks), const(lbp), const(gn2), s0spec],
        out_specs=[pl.BlockSpec((bb, C, W), lambda b, c: (*omap(b, c), 0)), sspec],
        out_shape=[jax.ShapeDtypeStruct((o_rows, z.shape[1], W), F32), jax.ShapeDtypeStruct(s0.shape[1:], F32)],
        compiler_params=_params("parallel", "arbitrary"),
        name="hgrn_mixer",
    )(z, z, z, z, jnp.asarray(wcat), jnp.asarray(masks), lbp, gn2, s0)


def _pool_linear(pooled, g, w_ref, b_ref, s_ref):
    sl = slice(g * POOL_GROUP_DIM, (g + 1) * POOL_GROUP_DIM)
    y = jnp.dot(pooled.astype(BF16), w_ref[g], preferred_element_type=F32) + b_ref[:, sl]
    return y * s_ref[:, sl]


def _pool_seq_kernel(cur_ref, halo_ref, w_ref, b_ref, s_ref, o_ref, *, tm, blocks_per_seq):
    blk = pl.program_id(0) % blocks_per_seq
    halo = jnp.where(blk == 0, 0.0, halo_ref[...])
    cur = cur_ref[...]
    full = jnp.concatenate([halo, cur], axis=0)
    pos = blk * tm + lax.broadcasted_iota(jnp.int32, (tm, 1), 0)
    for g, w in enumerate(POOL_WINDOWS):
        sl = slice(g * POOL_GROUP_DIM, (g + 1) * POOL_GROUP_DIM)
        acc = full[:, sl]
        step = 1
        while step < w:
            acc = acc + pltpu.roll(acc, step, axis=0)
            step *= 2
        cnt = jnp.minimum(w, pos + 1).astype(F32)
        pooled = acc[POOL_HALO:] / cnt - cur[:, sl]
        o_ref[:, sl] = _pool_linear(pooled, g, w_ref, b_ref, s_ref)


def pool_mixer_seq(z, n_blocks, blocks_per_seq, pw, layer, pb, ps, tm):
    W = len(POOL_WINDOWS) * POOL_GROUP_DIM
    col = z.shape[1] // W - 1
    const = lambda a: pl.BlockSpec(a.shape, lambda i: (0,) * a.ndim)
    return pl.pallas_call(
        functools.partial(_pool_seq_kernel, tm=tm, blocks_per_seq=blocks_per_seq),
        grid=(n_blocks,),
        in_specs=[pl.BlockSpec((tm, W), lambda i: (i, col)),
                  pl.BlockSpec((POOL_HALO, W), lambda i: (jnp.maximum(i * (tm // POOL_HALO) - 1, 0), col)),
                  pl.BlockSpec((None,) + pw.shape[1:], lambda i: (layer, 0, 0, 0)), const(pb), const(ps)],
        out_specs=pl.BlockSpec((tm, W), lambda i: (i, 0)),
        out_shape=jax.ShapeDtypeStruct((n_blocks * tm, W), F32),
        compiler_params=_params("parallel"),
        name="pool_mixer_seq",
    )(z, z, pw, pb, ps)


def _pool_step_kernel(full_ref, w_ref, b_ref, s_ref, o_ref, *, T, start_pos):
    for t in range(T):
        for g, w in enumerate(POOL_WINDOWS):
            sl = slice(g * POOL_GROUP_DIM, (g + 1) * POOL_GROUP_DIM)
            cur = full_ref[POOL_BUF + t, :, sl]
            acc = cur
            for j in range(1, w):
                acc = acc + full_ref[POOL_BUF + t - j, :, sl]
            pooled = acc / float(min(w, start_pos + t + 1)) - cur
            o_ref[t, :, sl] = _pool_linear(pooled, g, w_ref, b_ref, s_ref)


def pool_mixer_step(full_t, T, start_pos, pw, layer, pb, ps, bb):
    R, B, W = full_t.shape
    const = lambda a: pl.BlockSpec(a.shape, lambda i: (0,) * a.ndim)
    return pl.pallas_call(
        functools.partial(_pool_step_kernel, T=T, start_pos=start_pos),
        grid=(B // bb,),
        in_specs=[pl.BlockSpec((R, bb, W), lambda i: (0, i, 0)),
                  pl.BlockSpec((None,) + pw.shape[1:], lambda i: (layer, 0, 0, 0)), const(pb), const(ps)],
        out_specs=pl.BlockSpec((T, bb, W), lambda i: (0, i, 0)),
        out_shape=jax.ShapeDtypeStruct((T, B, W), F32),
        compiler_params=_params("parallel"),
        name="pool_mixer_step",
    )(full_t, pw, pb, ps)


def _oproj_kernel(x_ref, oap_ref, oad_ref, obp_ref, obd_ref, w_ref, gq_ref, gt_ref, o_ref, *, n_prompt_blocks):
    decode = pl.program_id(0) >= n_prompt_blocks
    wa = w_ref.shape[0] // 2
    oa = jnp.where(decode, oad_ref[...], oap_ref[...]).astype(BF16)
    ob = jnp.where(decode, obd_ref[...], obp_ref[...]).astype(BF16)
    mix = jnp.dot(oa, w_ref[0:wa, :], preferred_element_type=F32)
    mix = mix + jnp.dot(ob, w_ref[wa:, :], preferred_element_type=F32)
    o_ref[...] = x_ref[...] + _mod(gq_ref, gt_ref, decode) * mix


def out_proj_residual(st, x, oa_p, oa_d, ob_p, ob_d, w, layer, gate, tn):
    N, D = x.shape
    Wa = oa_p.shape[1]
    last = st.n_prompt_blocks - 1
    pspec = pl.BlockSpec((st.tm, Wa), lambda i, j: (jnp.minimum(i, last), 0))
    dspec = pl.BlockSpec((st.tm, Wa), lambda i, j: (0, 0))
    return pl.pallas_call(
        functools.partial(_oproj_kernel, n_prompt_blocks=st.n_prompt_blocks),
        grid=(st.n_blocks, D // tn),
        in_specs=[pl.BlockSpec((st.tm, tn), lambda i, j: (i, j)),
                  pspec, dspec, pspec, dspec,
                  pl.BlockSpec((None, 2 * Wa, tn), lambda i, j: (layer, 0, j)),
                  *_mod_specs(st, tn, col=True)],
        out_specs=pl.BlockSpec((st.tm, tn), lambda i, j: (i, j)),
        out_shape=jax.ShapeDtypeStruct((N, D), F32),
        compiler_params=_params("parallel", "arbitrary"),
        name="out_proj_residual",
    )(x, oa_p, oa_d, ob_p, ob_d, w, *gate)


def _peer_query_kernel(x_ref, g_ref, shq_ref, sht_ref, scq_ref, sct_ref, wqT_ref, hT_ref, qnT_ref, hT_scr,
                       *, n_prompt_blocks):
    @pl.when(pl.program_id(1) == 0)
    def _():
        decode = pl.program_id(0) >= n_prompt_blocks
        y = _rmsnorm(x_ref[...], g_ref[...])
        h = y * (1.0 + _mod(scq_ref, sct_ref, decode)) + _mod(shq_ref, sht_ref, decode)
        hT_scr[...] = h.T.astype(BF16)
        hT_ref[...] = hT_scr[...]

    q = jnp.dot(wqT_ref[...], hT_scr[...], preferred_element_type=F32)
    rows, tm = q.shape
    q3 = q.reshape(rows // PEER_KEYS, PEER_KEYS, tm)
    qn = q3 * lax.rsqrt(jnp.mean(q3 * q3, axis=1, keepdims=True) + EPS)
    qnT_ref[...] = qn.reshape(rows, tm).astype(BF16)


def peer_query(st, x, g, sh, sc, wqT, layer, tr=256):
    N, D = x.shape
    Q = wqT.shape[1]
    return pl.pallas_call(
        functools.partial(_peer_query_kernel, n_prompt_blocks=st.n_prompt_blocks),
        grid=(st.n_blocks, Q // tr),
        in_specs=[pl.BlockSpec((st.tm, D), lambda i, j: (i, 0)),
                  pl.BlockSpec((1, D), lambda i, j: (0, 0)),
                  *_mod_specs(st, D), *_mod_specs(st, D),
                  pl.BlockSpec((None, tr, D), lambda i, j: (layer, j, 0))],
        out_specs=[pl.BlockSpec((D, st.tm), lambda i, j: (0, i)),
                   pl.BlockSpec((tr, st.tm), lambda i, j: (j, i))],
        out_shape=[jax.ShapeDtypeStruct((D, N), BF16), jax.ShapeDtypeStruct((Q, N), BF16)],
        scratch_shapes=[pltpu.VMEM((D, st.tm), BF16)],
        compiler_params=_params("parallel", "arbitrary"),
        name="peer_query",
    )(x, g.reshape(1, D), *sh, *sc, wqT)


def _matmul_kernel(a_ref, b_ref, o_ref):
    o_ref[...] = jnp.dot(a_ref[...], b_ref[...], preferred_element_type=F32)


def peer_scores(kmat, layer, qnT, tr, tm):
    _, M, K = kmat.shape
    N = qnT.shape[1]
    return pl.pallas_call(
        _matmul_kernel,
        grid=(N // tm, M // tr),
        in_specs=[pl.BlockSpec((None, tr, K), lambda i, j: (layer, j, 0)),
                  pl.BlockSpec((K, tm), lambda i, j: (0, i))],
        out_specs=pl.BlockSpec((tr, tm), lambda i, j: (j, i)),
        out_shape=jax.ShapeDtypeStruct((M, N), F32),
        compiler_params=_params("parallel", "arbitrary"),
        name="peer_scores",
    )(kmat, qnT)


def _sort_pairs(n):
    pairs = []
    p = 1
    while p < n:
        k = p
        while k >= 1:
            for j in range(k % p, n - k, 2 * k):
                for i in range(min(k, n - j - k)):
                    if (i + j) // (2 * p) == (i + j + k) // (2 * p):
                        pairs.append((i + j, i + j + k))
            k //= 2
        p *= 2
    return pairs


_SORT16 = _sort_pairs(PEER_TOPK)


def _sort16_desc(xs):
    xs = list(xs)
    for i, j in _SORT16:
        xs[i], xs[j] = jnp.maximum(xs[i], xs[j]), jnp.minimum(xs[i], xs[j])
    return xs


def _merge_top(a, b):
    n = len(a)
    c = [jnp.maximum(a[i], b[n - 1 - i]) for i in range(n)]
    stride = n // 2
    while stride >= 1:
        for i in range(n):
            if not i & stride:
                c[i], c[i + stride] = jnp.maximum(c[i], c[i + stride]), jnp.minimum(c[i], c[i + stride])
        stride //= 2
    return c


def _top16_desc(blocks):
    while len(blocks) > 1:
        blocks = [_merge_top(blocks[i], blocks[i + 1]) for i in range(0, len(blocks), 2)]
    return blocks[0]


def _count(mask):
    return jnp.where(mask, 1.0, 0.0)


def _rank_in_sorted(v, x):
    c8 = v[7] > x
    c4 = jnp.where(c8, v[11], v[3]) > x
    c2 = jnp.where(c8, jnp.where(c4, v[13], v[9]), jnp.where(c4, v[5], v[1])) > x
    t1 = jnp.where(c8,
                   jnp.where(c4, jnp.where(c2, v[14], v[12]), jnp.where(c2, v[10], v[8])),
                   jnp.where(c4, jnp.where(c2, v[6], v[4]), jnp.where(c2, v[2], v[0])))
    c1 = t1 > x
    return (jnp.where(c8, 8.0, 0.0) + jnp.where(c4, 4.0, 0.0) + jnp.where(c2, 2.0, 0.0) + _count(c1)
            + _count(v[15] > x))


_DIRECT_COLS = 4
_LONG_ROWS = 3


def _select_fast(s_ref, cnt_ref, e1_ref, rank_ref, e2_ref):
    K, H, L = cnt_ref.shape
    k = PEER_TOPK
    s1 = [s_ref[pl.ds(a * H, H), :] for a in range(K)]
    s2 = [s_ref[pl.ds((K + b) * H, H), :] for b in range(K)]
    v1 = _top16_desc([_sort16_desc(s1[i:i + k]) for i in range(0, K, k)])
    v2 = _top16_desc([_sort16_desc(s2[i:i + k]) for i in range(0, K, k)])

    row = lambda i: [v1[i] + v2[j] for j in range(k // (i + 1))]
    rest = [c for i in range(1, k) for c in row(i)]
    rest = rest + [jnp.full((H, L), -jnp.inf, F32)] * (-len(rest) % k)
    top = _top16_desc([row(0)] + [_sort16_desc(rest[i:i + k]) for i in range(0, len(rest), k)])
    tau = top[k - 1]
    inv_z = 1.0 / sum(jnp.exp(t - top[0]) for t in top)
    extra = [jnp.maximum(sum(_count(c >= tau) for c in row(i)) - _DIRECT_COLS, 0.0) for i in range(_LONG_ROWS)]

    n1 = n2 = ncnt = jnp.zeros((H, L), F32)
    for a in range(K):
        x = s1[a]
        c = sum(_count(x + v2[j] >= tau) for j in range(_DIRECT_COLS))
        c = c + sum(jnp.where(x == v1[i], extra[i], 0.0) for i in range(_LONG_ROWS))
        cnt_ref[a] = c
        e1_ref[a] = jnp.exp(x - v1[0]) * inv_z
        ncnt = ncnt + c
        n1 = n1 + _count(x >= v1[k - 1])
    for b in range(K):
        x = s2[b]
        rank_ref[b] = _rank_in_sorted(v2, x)
        e2_ref[b] = jnp.exp(x - v2[0])
        n2 = n2 + _count(x >= v2[k - 1])
    return _count((n1 != k) | (n2 != k) | (ncnt != k))


def _extract_topk(cur_ref, pos_ref, val_ref, rounds):
    K = cur_ref.shape[0]
    kio = lax.broadcasted_iota(jnp.int32, cur_ref.shape, 0)
    pos_ref[...] = jnp.full(pos_ref.shape, rounds, jnp.int32)

    def one_round(r, carry):
        cur = cur_ref[...]
        m = jnp.max(cur, axis=0)
        first = jnp.min(jnp.where(cur == m[None], kio, K), axis=0)
        hit = kio == first[None]
        pos_ref[...] = jnp.where(hit, r, pos_ref[...])
        cur_ref[...] = jnp.where(hit, -jnp.inf, cur)
        val_ref[r] = m
        return carry

    lax.fori_loop(0, rounds, one_round, 0)


_CAND = [(i, j) for i in range(PEER_TOPK) for j in range(PEER_TOPK) if (i + 1) * (j + 1) <= PEER_TOPK]


def _select_exact(s_ref, cnt_ref, e1_ref, rank_ref, e2_ref,
                  cur_scr, pos1_scr, pos2_scr, v1_scr, v2_scr, cand_scr, posc_scr, vc_scr):
    K, H, L = cnt_ref.shape
    s1 = s_ref[0:K * H, :].reshape(K, H, L)
    s2 = s_ref[K * H:, :].reshape(K, H, L)
    cur_scr[...] = s1
    _extract_topk(cur_scr, pos1_scr, v1_scr, PEER_TOPK)
    cur_scr[...] = s2
    _extract_topk(cur_scr, pos2_scr, v2_scr, PEER_TOPK)

    for c, (i, j) in enumerate(_CAND):
        cand_scr[c] = v1_scr[i] + v2_scr[j]
    _extract_topk(cand_scr, posc_scr, vc_scr, PEER_TOPK)

    top = vc_scr[...]
    inv_z = 1.0 / jnp.sum(jnp.exp(top - top[0:1]), axis=0)
    pos1 = pos1_scr[...]
    cnt = jnp.zeros((K, H, L), F32)
    for i in range(PEER_TOPK):
        cs = [c for c, (ci, _) in enumerate(_CAND) if ci == i]
        n_i = sum((posc_scr[c] < PEER_TOPK).astype(F32) for c in cs)
        cnt = cnt + jnp.where(pos1 == i, n_i[None], 0.0)
    cnt_ref[...] = cnt
    e1_ref[...] = jnp.exp(s1 - v1_scr[0:1]) * inv_z[None]
    rank_ref[...] = pos2_scr[...].astype(F32)
    e2_ref[...] = jnp.exp(s2 - v2_scr[0:1])


def _peer_select_kernel(s_ref, cnt_ref, e1_ref, rank_ref, e2_ref, *scratch):
    unsure = _select_fast(s_ref, cnt_ref, e1_ref, rank_ref, e2_ref)

    @pl.when(jnp.max(unsure) > 0.0)
    def _():
        _select_exact(s_ref, cnt_ref, e1_ref, rank_ref, e2_ref, *scratch)


def peer_select(scoresT, tl=128):
    rows, N = scoresT.shape
    K, H = PEER_KEYS, PEER_HEADS
    out = jax.ShapeDtypeStruct((K, H, N), F32)
    ospec = pl.BlockSpec((K, H, tl), lambda i: (0, 0, i))
    nc = len(_CAND)
    return pl.pallas_call(
        _peer_select_kernel,
        grid=(N // tl,),
        in_specs=[pl.BlockSpec((rows, tl), lambda i: (0, i))],
        out_specs=[ospec] * 4,
        out_shape=[out] * 4,
        scratch_shapes=[pltpu.VMEM((K, H, tl), F32), pltpu.VMEM((K, H, tl), jnp.int32),
                        pltpu.VMEM((K, H, tl), jnp.int32), pltpu.VMEM((PEER_TOPK, H, tl), F32),
                        pltpu.VMEM((PEER_TOPK, H, tl), F32), pltpu.VMEM((nc, H, tl), F32),
                        pltpu.VMEM((nc, H, tl), jnp.int32), pltpu.VMEM((PEER_TOPK, H, tl), F32)],
        compiler_params=_params("parallel"),
        name="peer_select",
    )(scoresT)


def _gelu(x):
    return 0.5 * x * (1.0 + lax.erf(x * np.float32(1.0 / np.sqrt(2.0))))


def _peer_expert_kernel(u_ref, v_ref, hT_ref, cnt_ref, e1_ref, rank_ref, e2_ref, o_ref, a_scr, *, te, nj):
    s = pl.program_id(0)
    prev = jnp.maximum(s - 1, 0)
    cur_slot = s % 2

    @pl.when((s == 0) | (prev % nj == 0))
    def _():
        o_ref[...] = jnp.zeros_like(o_ref)

    @pl.when(s == 0)
    def _():
        a_scr[1] = jnp.zeros(a_scr.shape[1:], a_scr.dtype)

    sT = jnp.dot(u_ref[...], hT_ref[...], preferred_element_type=F32)
    o_ref[...] += lax.dot_general(a_scr[1 - cur_slot], v_ref[...], (((0,), (0,)), ((), ())),
                                  preferred_element_type=F32)
    act = _gelu(sT).astype(BF16)
    tm = sT.shape[1]
    zero = jnp.zeros((), BF16)
    for al in range(te // PEER_KEYS):
        a = (s % nj) * (te // PEER_KEYS) + al
        w = jnp.zeros((PEER_KEYS, tm), BF16)
        for h in range(PEER_HEADS):
            c = cnt_ref[h, pl.ds(a, 1), :].astype(BF16)
            e1 = e1_ref[h, pl.ds(a, 1), :].astype(BF16)
            w = w + jnp.where(rank_ref[h] < c, e2_ref[h], zero) * e1
        a_scr[cur_slot, pl.ds(al * PEER_KEYS, PEER_KEYS), :] = act[al * PEER_KEYS:(al + 1) * PEER_KEYS, :] * w


def peer_experts(u, v, layer, hT, cnt, e1, rank, e2, tm, te):
    _, E, D = u.shape
    N = hT.shape[1]
    H, K, _ = cnt.shape
    ni, nj = N // tm, E // te
    tok = lambda s: jnp.minimum(s // nj, ni - 1)
    sel = pl.BlockSpec((H, K, tm), lambda s: (0, 0, tok(s)))
    return pl.pallas_call(
        functools.partial(_peer_expert_kernel, te=te, nj=nj),
        grid=(ni * nj + 1,),
        in_specs=[pl.BlockSpec((None, te, D), lambda s: (layer, s % nj, 0)),
                  pl.BlockSpec((None, te, D), lambda s: (layer, jnp.maximum(s - 1, 0) % nj, 0)),
                  pl.BlockSpec((D, tm), lambda s: (0, tok(s))),
                  sel, sel, sel, sel],
        out_specs=pl.BlockSpec((tm, D), lambda s: (jnp.maximum(s - 1, 0) // nj, 0)),
        out_shape=jax.ShapeDtypeStruct((N, D), F32),
        scratch_shapes=[pltpu.VMEM((2, te, tm), BF16)],
        compiler_params=_params("arbitrary"),
        name="peer_experts",
    )(u, v, hT, cnt, e1, rank, e2)


def _peer_residual_kernel(x_ref, p_ref, gq_ref, gt_ref, o_ref, *, n_prompt_blocks):
    decode = pl.program_id(0) >= n_prompt_blocks
    o_ref[...] = x_ref[...] + _mod(gq_ref, gt_ref, decode) * p_ref[...]


def peer_residual(st, x, p, gate):
    N, D = x.shape
    blk = pl.BlockSpec((st.tm, D), lambda i: (i, 0))
    return pl.pallas_call(
        functools.partial(_peer_residual_kernel, n_prompt_blocks=st.n_prompt_blocks),
        grid=(st.n_blocks,),
        in_specs=[blk, blk, *_mod_specs(st, D)],
        out_specs=blk,
        out_shape=jax.ShapeDtypeStruct((N, D), F32),
        compiler_params=_params("parallel"),
        name="peer_residual",
    )(x, p, *gate)


def _final_kernel(x_ref, g_ref, shq_ref, sht_ref, scq_ref, sct_ref, op_ref, od_ref, *, n_prompt_blocks):
    decode = pl.program_id(0) >= n_prompt_blocks
    y = _rmsnorm(x_ref[...], g_ref[...]) * (1.0 + _mod(scq_ref, sct_ref, decode)) + _mod(shq_ref, sht_ref, decode)

    @pl.when(jnp.logical_not(decode))
    def _():
        op_ref[...] = y

    @pl.when(decode)
    def _():
        od_ref[...] = y


def final_norm(st, x, g, sh, sc):
    N, D = x.shape
    last = st.n_prompt_blocks - 1
    return pl.pallas_call(
        functools.partial(_final_kernel, n_prompt_blocks=st.n_prompt_blocks),
        grid=(st.n_blocks,),
        in_specs=[pl.BlockSpec((st.tm, D), lambda i: (i, 0)), pl.BlockSpec((1, D), lambda i: (0, 0)),
                  *_mod_specs(st, D), *_mod_specs(st, D)],
        out_specs=[pl.BlockSpec((st.tm, D), lambda i: (jnp.minimum(i, last), 0)),
                   pl.BlockSpec((st.tm, D), lambda i: (0, 0))],
        out_shape=[jax.ShapeDtypeStruct((st.n_prompt_blocks * st.tm, D), F32),
                   jax.ShapeDtypeStruct((st.tm, D), F32)],
        compiler_params=_params("arbitrary"),
        name="final_norm",
    )(x, g.reshape(1, D), *sh, *sc)


def _key_matrix(keys):
    Ly, H, P, K, Dh = keys.shape
    m = jnp.einsum('lhpkd,hg,pq->lpkhgqd', keys, jnp.eye(H, dtype=keys.dtype), jnp.eye(P, dtype=keys.dtype))
    return m.reshape(Ly, P * K * H, H * P * Dh)


def _peer_ffn(st, x, mod, norm_g, wqT, kmat, u, v, layer):
    sh2, sc2, g2 = mod
    hT, qnT = peer_query(st, x, norm_g, sh2, sc2, wqT, layer)
    scoresT = peer_scores(kmat, layer, qnT, 512, st.tm)
    cnt, e1, rank, e2 = (jnp.transpose(a, (1, 0, 2)) for a in peer_select(scoresT))
    p = peer_experts(u, v, layer, hT, cnt, e1, rank.astype(BF16), e2.astype(BF16), st.tm, 1024)
    return peer_residual(st, x, p, g2)


def _stream_mod(mod, n_mod, n_seq, n_tok, tok_per_seq):
    parts = jnp.split(mod, n_mod, axis=-1)
    return [(p[:n_seq, None, :], jnp.repeat(p[n_seq:n_seq + n_tok // tok_per_seq], tok_per_seq, axis=0))
            for p in parts]


def kernel(x_prompt, x_sample, c_prompt, c_sample, state_hgrn, state_pool, w_ada, b_ada, norm1_g, norm2_g, w_in, w_out, lb_logits, hgrn_norm_g, pool_w, pool_b, pool_scale, peer_wq, peer_keys, peer_u, peer_v, final_g, w_ada_final, b_ada_final):
    B, T, D = x_prompt.shape
    Bs, Ts, _ = x_sample.shape
    depth = w_ada.shape[0]
    Wh = HGRN_HEADS * HEAD_DIM
    C = 64
    Tpad = 8
    n_dec = Bs * Ts
    st = Stream(n_dec, B, T // n_dec)
    tm = st.tm

    lb = jnp.cumsum(jax.nn.softmax(lb_logits.astype(F32), axis=0), axis=0)
    lb = lb - lb[0:1]

    c_all = jnp.concatenate([c_prompt, c_sample], axis=0)
    c_all = jnp.pad(c_all, ((0, (-c_all.shape[0]) % 8), (0, 0)))
    split_mod = lambda mod, n: _stream_mod(mod, n, B, n_dec, Ts)

    w_in_b = w_in.astype(BF16)
    w_out_b = w_out.astype(BF16)
    wqT = jnp.swapaxes(peer_wq, 1, 2).astype(BF16)
    kmat = _key_matrix(peer_keys).astype(BF16)
    u_b = peer_u.astype(BF16)
    v_b = peer_v.astype(BF16)
    pw = pool_w.astype(BF16)

    x = jnp.concatenate([x_prompt.reshape(B * T, D), x_sample.reshape(n_dec, D)], axis=0)
    zero_state = jnp.zeros((1, B, HGRN_HEADS, HEAD_DIM, HEAD_DIM), F32)
    sp_list, bp_list, ss_list, bs_list = [], [], [], []
    cps = tm // C

    for l in range(depth):
        mod = split_mod(ada_mod(c_all, w_ada, l, b_ada[l]), 6)
        pb = pool_b[l].reshape(1, -1)
        ps = pool_scale[l].reshape(1, -1)
        lbp = jnp.stack([jnp.log(lb[l]), jnp.log1p(-lb[l]), 1.0 - lb[l]])

        z = norm_mod_matmul(st, x, norm1_g[l], mod[0], mod[1], w_in_b, l, 1024)
        z3 = z.reshape(st.n_blocks, tm, -1)

        oa_p, sp = hgrn_mixer(
            z3, lambda b, c: (b * st.blocks_per_seq + c // cps, c % cps), lambda b, c: (b * st.blocks_per_seq + c // cps, c % cps),
            st.n_prompt_blocks, zero_state, 0, lbp, hgrn_norm_g[l], grid=(B, T // C), bb=1, C=C, t_valid=C)
        ob_p = pool_mixer_seq(z, st.n_prompt_blocks, st.blocks_per_seq, pw, l, pb, ps, tm)
        sp_list.append(sp)
        bp_list.append(z[:B * T].reshape(B, T, -1)[:, T - POOL_BUF:, 4 * Wh:])

        zd = z[B * T:].reshape(Bs, Ts, -1)
        zpad = jnp.pad(zd, ((0, 0), (0, Tpad - Ts), (0, 0)))
        oa_d, ss = hgrn_mixer(zpad, lambda b, c: (b, 0), lambda b, c: (b, 0), Bs, state_hgrn, l, lbp, hgrn_norm_g[l],
                              grid=(Bs // 8, 1), bb=8, C=Tpad, t_valid=Ts)
        full = jnp.concatenate([state_pool[l], zd[:, :, 4 * Wh:]], axis=1)
        ob_d = pool_mixer_step(jnp.transpose(full, (1, 0, 2)), Ts, PAST_LEN, pw, l, pb, ps, 32)
        ob_d = jnp.transpose(ob_d, (1, 0, 2)).reshape(n_dec, -1)
        ss_list.append(ss)
        bs_list.append(full[:, Ts:])

        x = out_proj_residual(st, x, oa_p.reshape(B * T, Wh), oa_d[:, :Ts].reshape(n_dec, Wh), ob_p, ob_d,
                              w_out_b, l, mod[2], 1024)
        x = _peer_ffn(st, x, mod[3:6], norm2_g[l], wqT, kmat, u_b, v_b, l)

    modf = split_mod(ada_mod(c_all, w_ada_final[None], 0, b_ada_final), 2)
    y_prompt, y_sample = final_norm(st, x, final_g, modf[0], modf[1])
    return (y_prompt.reshape(B, T, D), y_sample.reshape(Bs, Ts, D),
            jnp.stack(sp_list), jnp.stack(bp_list), jnp.stack(ss_list), jnp.stack(bs_list))
```
